```python
import math
import jax, jax.numpy as jnp
from jax import lax
import numpy as np

D_MODEL = 2048
BATCH = 4
SEQ = 2048
DEPTH = 2
DEC_BATCH = 32
DEC_SEQ = 8
PAST_LEN = 8192
PAGE_SIZE = 128

N_A_LAYERS = DEPTH // 2
N_B_LAYERS = DEPTH - N_A_LAYERS
CONV_W = 3
N_HEADS = 16
HEAD_DIM = D_MODEL // N_HEADS
D_FF = 4 * D_MODEL
Q_BLOCK = 128
RMS_EPS = 1e-6
FORGET_BIAS_MEAN = 5.0
FORGET_W_SCALE = 0.5

kernel_name = "yoco_shortconv_fox_step"


def rms_norm(x, g):
    x32 = x.astype(jnp.float32)
    y = x32 * lax.rsqrt(jnp.mean(x32 * x32, axis=-1, keepdims=True) + RMS_EPS)
    return (y * g.astype(jnp.float32)).astype(x.dtype)


def short_conv_mixer(x, conv_prev, w_in, w_conv, w_out):
    t = x.shape[1]
    b_gate, c_gate, x_in = jnp.split(x @ w_in, 3, axis=-1)
    u = c_gate * x_in
    u_ext = jnp.concatenate([conv_prev.astype(u.dtype), u], axis=1)
    conv = sum(w_conv[k] * u_ext[:, k:k + t] for k in range(CONV_W))
    y = (b_gate * conv) @ w_out
    return y, u_ext[:, -(CONV_W - 1):]


def squared_relu_mlp(x, w_up, w_down):
    return jnp.square(jax.nn.relu(x @ w_up)) @ w_down


def shared_kv(h, g_kv, w_kvf, b_f):
    b, t, _ = h.shape
    proj = rms_norm(h, g_kv) @ w_kvf
    k = proj[..., :D_MODEL].reshape(b, t, N_HEADS, HEAD_DIM)
    v = proj[..., D_MODEL:2 * D_MODEL].reshape(b, t, N_HEADS, HEAD_DIM)
    lf = jax.nn.log_sigmoid((proj[..., 2 * D_MODEL:] + b_f).astype(jnp.float32))
    return k, v, lf


def fox_attend(q, k, v, c_q, c_k, q_pos, k_pos):
    logits = jnp.einsum("bthd,bshd->bhts", q.astype(jnp.float32), k.astype(jnp.float32))
    logits = logits * (HEAD_DIM ** -0.5)
    bias = jnp.swapaxes(c_q, 1, 2)[..., :, None] - jnp.swapaxes(c_k, 1, 2)[..., None, :]
    mask = k_pos[None, :] <= q_pos[:, None]
    logits = jnp.where(mask, logits + bias, -jnp.inf)
    p = jax.nn.softmax(logits, axis=-1)
    return jnp.einsum("bhts,bshd->bthd", p, v.astype(jnp.float32)).astype(q.dtype)


def fox_prompt(q, k, v, lf):
    b, t, h, dh = q.shape
    c = jnp.cumsum(lf, axis=1)
    k_pos = jnp.arange(t)

    def block(i):
        start = i * Q_BLOCK
        q_b = lax.dynamic_slice_in_dim(q, start, Q_BLOCK, axis=1)
        c_b = lax.dynamic_slice_in_dim(c, start, Q_BLOCK, axis=1)
        q_pos = start + jnp.arange(Q_BLOCK)
        return fox_attend(q_b, k, v, c_b, c, q_pos, k_pos)

    out = lax.map(block, jnp.arange(t // Q_BLOCK))
    return jnp.swapaxes(out, 0, 1).reshape(b, t, h, dh)


def fox_sample(q, k_new, v_new, lf_new, k_past, v_past, lf_past):
    p_len = k_past.shape[1]
    t = q.shape[1]
    k = jnp.concatenate([k_past.astype(k_new.dtype), k_new], axis=1)
    v = jnp.concatenate([v_past.astype(v_new.dtype), v_new], axis=1)
    lf = jnp.concatenate([lf_past.astype(jnp.float32), lf_new], axis=1)
    c = jnp.cumsum(lf, axis=1)
    q_pos = p_len + jnp.arange(t)
    k_pos = jnp.arange(p_len + t)
    return fox_attend(q, k, v, c[:, p_len:], c, q_pos, k_pos)


def run_trunk(h, conv_prev, kv_past, w_in_a, w_conv_a, w_out_a, g_kv, w_kvf, b_f,
              w_q, w_o, g_mix_pre, g_mix_post, g_ffn_pre, g_ffn_post, w_up, w_down):
    b, t, _ = h.shape
    conv_new = []
    shared = None
    for layer in range(DEPTH):
        a = rms_norm(h, g_mix_pre[layer])
        if layer < N_A_LAYERS:
            mix, st = short_conv_mixer(a, conv_prev[layer], w_in_a[layer],
                                       w_conv_a[layer], w_out_a[layer])
            conv_new.append(st)
        else:
            j = layer - N_A_LAYERS
            q = (a @ w_q[j]).reshape(b, t, N_HEADS, HEAD_DIM)
            if kv_past is None:
                o = fox_prompt(q, *shared)
            else:
                o = fox_sample(q, *shared, *kv_past)
            mix = o.reshape(b, t, D_MODEL) @ w_o[j]
        h = h + rms_norm(mix, g_mix_post[layer])
        f = squared_relu_mlp(rms_norm(h, g_ffn_pre[layer]), w_up[layer], w_down[layer])
        h = h + rms_norm(f, g_ffn_post[layer])
        if layer == N_A_LAYERS - 1:
            shared = shared_kv(h, g_kv, w_kvf, b_f)
    return h, jnp.stack(conv_new, axis=0), shared


def setup_inputs(seed: int = 0) -> dict:
    key = jax.random.key(seed)
    ks = jax.random.split(key, 24)
    n_pages = PAST_LEN // PAGE_SIZE
    n_used = DEC_BATCH * n_pages
    n_pool = n_used + n_used // 4
    nrm = jax.random.normal
    sd = D_MODEL ** -0.5

    x_prompt = nrm(ks[0], (BATCH, SEQ, D_MODEL), jnp.float32)
    x_sample = nrm(ks[1], (DEC_BATCH, DEC_SEQ, D_MODEL), jnp.float32)
    state_conv = nrm(ks[2], (N_A_LAYERS, DEC_BATCH, CONV_W - 1, D_MODEL), jnp.float32)
    cache_k = nrm(ks[3], (n_pool, PAGE_SIZE, N_HEADS, HEAD_DIM), jnp.float32)
    cache_v = nrm(ks[4], (n_pool, PAGE_SIZE, N_HEADS, HEAD_DIM), jnp.float32)
    cache_logf = jax.nn.log_sigmoid(
        FORGET_BIAS_MEAN + 0.5 * nrm(ks[5], (n_pool, PAGE_SIZE, N_HEADS), jnp.float32))
    page_table = jax.random.permutation(ks[6], n_pool)[:n_used].reshape(
        DEC_BATCH, n_pages).astype(jnp.int32)

    w_in_a = nrm(ks[7], (N_A_LAYERS, D_MODEL, 3 * D_MODEL), jnp.float32) * sd
    w_conv_a = nrm(ks[8], (N_A_LAYERS, CONV_W, D_MODEL), jnp.float32) * CONV_W ** -0.5
    w_out_a = nrm(ks[9], (N_A_LAYERS, D_MODEL, D_MODEL), jnp.float32) * sd
    g_kv = 1.0 + 0.1 * nrm(ks[10], (D_MODEL,), jnp.float32)
    w_kv = nrm(ks[11], (D_MODEL, 2 * D_MODEL), jnp.float32) * sd
    w_f = nrm(ks[12], (D_MODEL, N_HEADS), jnp.float32) * (FORGET_W_SCALE * sd)
    w_kvf = jnp.concatenate([w_kv, w_f], axis=1)
    b_f = FORGET_BIAS_MEAN + 0.5 * nrm(ks[13], (N_HEADS,), jnp.float32)
    w_q = nrm(ks[14], (N_B_LAYERS, D_MODEL, D_MODEL), jnp.float32) * sd
    w_o = nrm(ks[15], (N_B_LAYERS, D_MODEL, D_MODEL), jnp.float32) * sd
    g_mix_pre = 1.0 + 0.1 * nrm(ks[16], (DEPTH, D_MODEL), jnp.float32)
    g_mix_post = 1.0 + 0.1 * nrm(ks[17], (DEPTH, D_MODEL), jnp.float32)
    g_ffn_pre = 1.0 + 0.1 * nrm(ks[18], (DEPTH, D_MODEL), jnp.float32)
    g_ffn_post = 1.0 + 0.1 * nrm(ks[19], (DEPTH, D_MODEL), jnp.float32)
    w_up = nrm(ks[20], (DEPTH, D_MODEL, D_FF), jnp.float32) * sd
    w_down = nrm(ks[21], (DEPTH, D_FF, D_MODEL), jnp.float32) * D_FF ** -0.5
    return {"x_prompt": x_prompt, "x_sample": x_sample, "state_conv": state_conv,
            "cache_k": cache_k, "cache_v": cache_v, "cache_logf": cache_logf,
            "page_table": page_table, "w_in_a": w_in_a, "w_conv_a": w_conv_a,
            "w_out_a": w_out_a, "g_kv": g_kv, "w_kvf": w_kvf, "b_f": b_f,
            "w_q": w_q, "w_o": w_o, "g_mix_pre": g_mix_pre, "g_mix_post": g_mix_post,
            "g_ffn_pre": g_ffn_pre, "g_ffn_post": g_ffn_post, "w_up": w_up,
            "w_down": w_down}


def reference(x_prompt, x_sample, state_conv, cache_k, cache_v, cache_logf, page_table,
              w_in_a, w_conv_a, w_out_a, g_kv, w_kvf, b_f, w_q, w_o,
              g_mix_pre, g_mix_post, g_ffn_pre, g_ffn_post, w_up, w_down):
    weights = (w_in_a, w_conv_a, w_out_a, g_kv, w_kvf, b_f, w_q, w_o,
               g_mix_pre, g_mix_post, g_ffn_pre, g_ffn_post, w_up, w_down)

    conv0 = jnp.zeros((N_A_LAYERS, x_prompt.shape[0], CONV_W - 1, D_MODEL), x_prompt.dtype)
    y_prompt, conv_prompt, (k_prompt, v_prompt, logf_prompt) = run_trunk(
        x_prompt, conv0, None, *weights)

    bd, n_pages = page_table.shape
    page = cache_k.shape[1]
    k_past = cache_k[page_table].reshape(bd, n_pages * page, N_HEADS, HEAD_DIM)
    v_past = cache_v[page_table].reshape(bd, n_pages * page, N_HEADS, HEAD_DIM)
    lf_past = cache_logf[page_table].reshape(bd, n_pages * page, N_HEADS)
    y_sample, conv_sample, (k_sample, v_sample, logf_sample) = run_trunk(
        x_sample, state_conv, (k_past, v_past, lf_past), *weights)

    return (y_prompt, y_sample, conv_prompt, conv_sample,
            k_prompt, v_prompt, logf_prompt, k_sample, v_sample, logf_sample)
```

```python
import functools
import math

import jax
import jax.numpy as jnp
from jax import lax
from jax.experimental import pallas as pl
from jax.experimental.pallas import tpu as pltpu

RMS_EPS = 1e-6
CONV_W = 3
N_HEADS = 16
HEAD_DIM = 128
LANES = 128
SUBLANES = 8
VMEM_LIMIT = 56 * 1024 * 1024
NEG_BIG = -1e30

F32 = jnp.float32
BF16 = jnp.bfloat16


def _params(*sem):
    return pltpu.CompilerParams(dimension_semantics=sem, vmem_limit_bytes=VMEM_LIMIT)


def _rms(x, g):
    return x * lax.rsqrt(jnp.mean(x * x, axis=-1, keepdims=True) + RMS_EPS) * g


def _split3(x):
    a = x.astype(BF16)
    r = x - a.astype(F32)
    b = r.astype(BF16)
    c = (r - b.astype(F32)).astype(BF16)
    return a, b, c


def _dot_exact01(x, m01):
    a, b, c = _split3(x)
    d = functools.partial(jnp.dot, preferred_element_type=F32)
    return d(a, m01) + d(b, m01) + d(c, m01)


def _dot01_exact(m01, x):
    a, b, c = _split3(x)
    d = functools.partial(jnp.dot, preferred_element_type=F32)
    return d(m01, a) + d(m01, b) + d(m01, c)


def _inproj_conv_kernel(*refs, tm, tiles_per_seq, seq_len, inject):
    if inject:
        (x_ref, g_ref, wb_ref, wc_ref, wx_ref, wconv_ref, p1_ref, p2_ref,
         gated_ref, u_ref, a_s) = refs
    else:
        (x_ref, g_ref, wb_ref, wc_ref, wx_ref, wconv_ref,
         gated_ref, u_ref, a_s, carry_s) = refs
    i = pl.program_id(0)
    j = pl.program_id(1)

    @pl.when(j == 0)
    def _():
        a_s[...] = _rms(x_ref[...], g_ref[...]).astype(BF16)

    a = a_s[...]
    dot = functools.partial(jnp.dot, preferred_element_type=F32)
    u = dot(a, wc_ref[...]) * dot(a, wx_ref[...])
    rolled1 = pltpu.roll(u, 1, axis=0)
    rolled2 = pltpu.roll(u, 2, axis=0)
    row = lax.broadcasted_iota(jnp.int32, u.shape, 0)
    if inject:
        pos = row % seq_len
        u1 = jnp.where(pos >= 1, rolled1, p1_ref[...])
        u2 = jnp.where(pos >= 2, rolled2, p2_ref[...])
        u_ref[...] = u
    else:
        @pl.when(i % tiles_per_seq == 0)
        def _():
            carry_s[j] = jnp.zeros(carry_s.shape[1:], F32)

        tail = carry_s[j]
        t1 = tail[SUBLANES - 1:SUBLANES, :]
        t2 = tail[SUBLANES - 2:SUBLANES - 1, :]
        u1 = jnp.where(row == 0, t1, rolled1)
        u2 = jnp.where(row == 0, t2, jnp.where(row == 1, t1, rolled2))
        last = u[tm - SUBLANES:, :]
        carry_s[j] = last
        u_ref[...] = last
    wconv = wconv_ref[...]
    conv = wconv[0:1, :] * u2 + wconv[1:2, :] * u1 + wconv[2:3, :] * u
    gated_ref[...] = (dot(a, wb_ref[...]) * conv).astype(BF16)


def _inproj_conv(x, g, w_in, w_conv, *, tm, tn, seq_len, hist=None):
    n, d = x.shape
    nj = d // tn
    inject = hist is not None
    assert n % tm == 0 and d % tn == 0
    if not inject:
        assert seq_len % tm == 0
    x_spec = pl.BlockSpec((tm, d), lambda i, j: (i, 0))
    g_spec = pl.BlockSpec((1, d), lambda i, j: (0, 0))
    w_specs = [pl.BlockSpec((d, tn), lambda i, j, s=s: (0, j + s * nj)) for s in range(3)]
    wconv_spec = pl.BlockSpec((CONV_W, tn), lambda i, j: (0, j))
    tile_spec = pl.BlockSpec((tm, tn), lambda i, j: (i, j))
    in_specs = [x_spec, g_spec, *w_specs, wconv_spec]
    args = [x, g, w_in, w_in, w_in, w_conv]
    scratch = [pltpu.VMEM((tm, d), BF16)]
    if inject:
        in_specs += [tile_spec, tile_spec]
        args += list(hist)
        u_shape = jax.ShapeDtypeStruct((n, d), F32)
        u_spec = tile_spec
    else:
        scratch.append(pltpu.VMEM((nj, SUBLANES, tn), F32))
        u_shape = jax.ShapeDtypeStruct((n // tm * SUBLANES, d), F32)
        u_spec = pl.BlockSpec((SUBLANES, tn), lambda i, j: (i, j))
    kern = functools.partial(_inproj_conv_kernel, tm=tm, tiles_per_seq=max(seq_len // tm, 1),
                             seq_len=seq_len, inject=inject)
    return pl.pallas_call(
        kern,
        grid=(n // tm, nj),
        in_specs=in_specs,
        out_specs=[tile_spec, u_spec],
        out_shape=[jax.ShapeDtypeStruct((n, d), BF16), u_shape],
        scratch_shapes=scratch,
        compiler_params=_params("arbitrary", "arbitrary"),
        name="inproj_conv",
    )(*args)


def _outproj_kernel(a_ref, w_ref, h_ref, gpost_ref, gnext_ref, h1_ref, anext_ref):
    mix = jnp.dot(a_ref[...], w_ref[...], preferred_element_type=F32)
    h1 = h_ref[...] + _rms(mix, gpost_ref[...])
    h1_ref[...] = h1
    anext_ref[...] = _rms(h1, gnext_ref[...]).astype(BF16)


def _outproj_residual(a, w, h, g_post, g_next, *, tm):
    n, d = h.shape
    row = pl.BlockSpec((tm, d), lambda i: (i, 0))
    vec = pl.BlockSpec((1, d), lambda i: (0, 0))
    return pl.pallas_call(
        _outproj_kernel,
        grid=(n // tm,),
        in_specs=[row, pl.BlockSpec((d, d), lambda i: (0, 0)), row, vec, vec],
        out_specs=[row, row],
        out_shape=[jax.ShapeDtypeStruct((n, d), F32), jax.ShapeDtypeStruct((n, d), BF16)],
        compiler_params=_params("arbitrary"),
        name="outproj_residual",
    )(a, w, h, g_post, g_next)


def _mlp_kernel(*refs, n_next):
    a_ref, wup_ref, wdown_ref, h_ref, gpost_ref, gnext_ref = refs[:6]
    h2_ref = refs[6]
    next_refs = refs[7:7 + n_next]
    acc_s = refs[7 + n_next]
    f = pl.program_id(1)

    @pl.when(f == 0)
    def _():
        acc_s[...] = jnp.zeros(acc_s.shape, F32)

    hid = jnp.dot(a_ref[...], wup_ref[...], preferred_element_type=F32)
    hid = jnp.square(jnp.maximum(hid, 0.0)).astype(BF16)
    acc_s[...] += jnp.dot(hid, wdown_ref[...], preferred_element_type=F32)

    @pl.when(f == pl.num_programs(1) - 1)
    def _():
        h2 = h_ref[...] + _rms(acc_s[...], gpost_ref[...])
        h2_ref[...] = h2
        gnext = gnext_ref[...]
        for k, ref in enumerate(next_refs):
            ref[...] = _rms(h2, gnext[k:k + 1, :]).astype(BF16)


def _mlp_residual(a, w_up, w_down, h, g_post, g_next, *, tm, tf):
    n, d = h.shape
    d_ff = w_up.shape[1]
    n_next = 0 if g_next is None else g_next.shape[0]
    g_arr = jnp.ones((1, d), F32) if g_next is None else g_next
    row = pl.BlockSpec((tm, d), lambda i, f: (i, 0))
    outs = [jax.ShapeDtypeStruct((n, d), F32)] + [jax.ShapeDtypeStruct((n, d), BF16)] * n_next
    return pl.pallas_call(
        functools.partial(_mlp_kernel, n_next=n_next),
        grid=(n // tm, d_ff // tf),
        in_specs=[row,
                  pl.BlockSpec((d, tf), lambda i, f: (0, f)),
                  pl.BlockSpec((tf, d), lambda i, f: (f, 0)),
                  row,
                  pl.BlockSpec((1, d), lambda i, f: (0, 0)),
                  pl.BlockSpec(g_arr.shape, lambda i, f: (0, 0))],
        out_specs=[row] * (1 + n_next),
        out_shape=outs,
        scratch_shapes=[pltpu.VMEM((tm, d), F32)],
        compiler_params=_params("arbitrary", "arbitrary"),
        name="mlp_residual",
    )(a, w_up, w_down, h, g_post, g_arr)


def _proj_kernel(a_ref, w_ref, *out_refs, scale):
    y = jnp.dot(a_ref[...], w_ref[...], preferred_element_type=F32)
    if scale is not None:
        y = y * scale
    for ref in out_refs:
        ref[...] = y.astype(ref.dtype)


def _proj(a, w, *, col0, n_out, out_dtypes, tm, tn, scale=None):
    n, d = a.shape
    assert col0 % tn == 0 and n_out % tn == 0
    jb = col0 // tn
    tile = pl.BlockSpec((tm, tn), lambda i, j: (i, j))
    return pl.pallas_call(
        functools.partial(_proj_kernel, scale=scale),
        grid=(n // tm, n_out // tn),
        in_specs=[pl.BlockSpec((tm, d), lambda i, j: (i, 0)),
                  pl.BlockSpec((d, tn), lambda i, j: (0, j + jb))],
        out_specs=[tile] * len(out_dtypes),
        out_shape=[jax.ShapeDtypeStruct((n, n_out), dt) for dt in out_dtypes],
        compiler_params=_params("arbitrary", "arbitrary"),
        name="proj",
    )(a, w)


def _logf_kernel(a_ref, wf_ref, bf_ref, lf_ref, c_ref, carry_s, *, tm, tiles_per_seq, seq_len):
    i = pl.program_id(0)
    x = jnp.dot(a_ref[...], wf_ref[...], preferred_element_type=F32) + bf_ref[...]
    lf = jnp.minimum(x, 0.0) - jnp.log1p(jnp.exp(-jnp.abs(x)))
    lf_ref[...] = lf
    r = lax.broadcasted_iota(jnp.int32, (tm, tm), 0)
    c = lax.broadcasted_iota(jnp.int32, (tm, tm), 1)
    same_seq = (r // seq_len) == (c // seq_len) if seq_len < tm else (r >= 0)
    tri = jnp.where((c <= r) & same_seq, 1.0, 0.0).astype(BF16)
    csum = _dot01_exact(tri, lf)

    @pl.when(i % tiles_per_seq == 0)
    def _():
        carry_s[...] = jnp.zeros(carry_s.shape, F32)

    csum = csum + carry_s[0:1, :]
    c_ref[...] = csum
    carry_s[...] = jnp.broadcast_to(csum[tm - 1:tm, :], carry_s.shape)


def _logf(a, w_f, b_f, *, tm, seq_len):
    n, d = a.shape
    assert seq_len % tm == 0 or tm % seq_len == 0
    tile = pl.BlockSpec((tm, LANES), lambda i: (i, 0))
    kern = functools.partial(_logf_kernel, tm=tm, tiles_per_seq=max(seq_len // tm, 1), seq_len=seq_len)
    return pl.pallas_call(
        kern,
        grid=(n // tm,),
        in_specs=[pl.BlockSpec((tm, d), lambda i: (i, 0)),
                  pl.BlockSpec((d, LANES), lambda i: (0, 0)),
                  pl.BlockSpec((1, LANES), lambda i: (0, 0))],
        out_specs=[tile, tile],
        out_shape=[jax.ShapeDtypeStruct((n, LANES), F32)] * 2,
        scratch_shapes=[pltpu.VMEM((SUBLANES, LANES), F32)],
        compiler_params=_params("arbitrary"),
        name="logf",
    )(a, w_f, b_f)


def _fox_prompt_kernel(q_ref, k_ref, v_ref, cq_ref, ck_ref, o_ref, m_s, l_s, acc_s, *, tq, tk):
    qi = pl.program_id(1)
    ki = pl.program_id(2)

    @pl.when(ki == 0)
    def _():
        m_s[...] = jnp.full(m_s.shape, -jnp.inf, F32)
        l_s[...] = jnp.zeros(l_s.shape, F32)
        acc_s[...] = jnp.zeros(acc_s.shape, F32)

    @pl.when(ki <= qi)
    def _():
        q_pos = qi * tq + lax.broadcasted_iota(jnp.int32, (tq, tk), 0)
        k_pos = ki * tk + lax.broadcasted_iota(jnp.int32, (tq, tk), 1)
        visible = k_pos <= q_pos
        cq = cq_ref[0]
        ck = ck_ref[0]
        for h in range(N_HEADS):
            cols = slice(h * HEAD_DIM, (h + 1) * HEAD_DIM)
            s = lax.dot_general(q_ref[0, :, cols], k_ref[0, :, cols],
                                (((1,), (1,)), ((), ())), preferred_element_type=F32)
            s = s + (cq[:, h:h + 1] - ck[h:h + 1, :])
            s = jnp.where(visible, s, NEG_BIG)
            m_old = m_s[h][:, 0:1]
            m_new = jnp.maximum(m_old, jnp.max(s, axis=-1, keepdims=True))
            alpha = jnp.exp(m_old - m_new)
            p = jnp.exp(s - m_new)
            l_s[h] = jnp.broadcast_to(alpha * l_s[h][:, 0:1] + jnp.sum(p, axis=-1, keepdims=True),
                                      l_s.shape[1:])
            m_s[h] = jnp.broadcast_to(m_new, m_s.shape[1:])
            pv = jnp.dot(p.astype(BF16), v_ref[0, :, cols], preferred_element_type=F32)
            acc_s[:, cols] = alpha * acc_s[:, cols] + pv

    @pl.when(ki == pl.num_programs(2) - 1)
    def _():
        for h in range(N_HEADS):
            cols = slice(h * HEAD_DIM, (h + 1) * HEAD_DIM)
            o_ref[0, :, cols] = (acc_s[:, cols] / l_s[h][:, 0:1]).astype(o_ref.dtype)


def _fox_prompt(q, k, v, c_rows, c_heads, *, tq, tk):
    b, t, d = q.shape
    kv_map = lambda bi, qi, ki: (bi, jnp.minimum(ki, qi), 0)
    return pl.pallas_call(
        functools.partial(_fox_prompt_kernel, tq=tq, tk=tk),
        grid=(b, t // tq, t // tk),
        in_specs=[pl.BlockSpec((1, tq, d), lambda bi, qi, ki: (bi, qi, 0)),
                  pl.BlockSpec((1, tk, d), kv_map),
                  pl.BlockSpec((1, tk, d), kv_map),
                  pl.BlockSpec((1, tq, LANES), lambda bi, qi, ki: (bi, qi, 0)),
                  pl.BlockSpec((1, N_HEADS, tk), lambda bi, qi, ki: (bi, 0, jnp.minimum(ki, qi)))],
        out_specs=pl.BlockSpec((1, tq, d), lambda bi, qi, ki: (bi, qi, 0)),
        out_shape=jax.ShapeDtypeStruct((b, t, d), BF16),
        scratch_shapes=[pltpu.VMEM((N_HEADS, tq, LANES), F32),
                        pltpu.VMEM((N_HEADS, tq, LANES), F32),
                        pltpu.VMEM((tq, d), F32)],
        compiler_params=_params("arbitrary", "arbitrary", "arbitrary"),
        name="fox_prompt",
    )(q, k, v, c_rows, c_heads)


def _page_suffix_kernel(lf_ref, out_ref):
    pages, heads, page = lf_ref.shape
    x = lf_ref[...].reshape(pages * heads, page)
    r = lax.broadcasted_iota(jnp.int32, (page, page), 0)
    c = lax.broadcasted_iota(jnp.int32, (page, page), 1)
    suffix01 = jnp.where(r >= c, 1.0, 0.0).astype(BF16)
    out_ref[...] = _dot_exact01(x, suffix01).reshape(pages, heads, page)


def _page_suffix(logf_t, *, pages_per_step):
    n_pool, heads, page = logf_t.shape
    assert n_pool % pages_per_step == 0
    blk = pl.BlockSpec((pages_per_step, heads, page), lambda i: (i, 0, 0))
    return pl.pallas_call(
        _page_suffix_kernel,
        grid=(n_pool // pages_per_step,),
        in_specs=[blk],
        out_specs=blk,
        out_shape=jax.ShapeDtypeStruct(logf_t.shape, F32),
        compiler_params=_params("arbitrary"),
        name="page_suffix",
    )(logf_t)


def _fox_sample_kernel(pt_ref, q_ref, kn_ref, vn_ref, cn_ref, cnt_ref, *refs, group, page):
    k_refs = refs[:group]
    v_refs = refs[group:2 * group]
    s_refs = refs[2 * group:3 * group]
    o_ref = refs[3 * group]
    m_s, l_s, acc_s, carry_s = refs[3 * group + 1:]
    step = pl.program_id(1)
    t_new = q_ref.shape[1]
    cn = cn_ref[0]
    nt = (((1,), (1,)), ((), ()))

    @pl.when(step == 0)
    def _():
        cnt = cnt_ref[0]
        tq = lax.broadcasted_iota(jnp.int32, (t_new, t_new), 0)
        tk = lax.broadcasted_iota(jnp.int32, (t_new, t_new), 1)
        for h in range(N_HEADS):
            cols = slice(h * HEAD_DIM, (h + 1) * HEAD_DIM)
            s = lax.dot_general(q_ref[0, :, cols], kn_ref[0, :, cols], nt, preferred_element_type=F32)
            s = s + (cn[:, h:h + 1] - cnt[h:h + 1, :])
            s = jnp.where(tk <= tq, s, NEG_BIG)
            m = jnp.max(s, axis=-1, keepdims=True)
            p = jnp.exp(s - m)
            m_s[h] = jnp.broadcast_to(m, m_s.shape[1:])
            l_s[h] = jnp.broadcast_to(jnp.sum(p, axis=-1, keepdims=True), l_s.shape[1:])
            acc_s[h] = jnp.dot(p, vn_ref[0, :, cols], preferred_element_type=F32)
        carry_s[...] = jnp.zeros(carry_s.shape, F32)

    lane = lax.broadcasted_iota(jnp.int32, (N_HEADS, page), 1)
    carry = carry_s[...][:, 0:1]
    suffix = []
    for g in range(group):
        inc = s_refs[g][0]
        exc = jnp.where(lane < page - 1, pltpu.roll(inc, page - 1, axis=1), 0.0)
        suffix.append(carry + exc)
        carry = carry + inc[:, 0:1]
    carry_s[...] = jnp.broadcast_to(carry, carry_s.shape)

    for h in range(N_HEADS):
        cols = slice(h * HEAD_DIM, (h + 1) * HEAD_DIM)
        qh = q_ref[0, :, cols]
        s_list = []
        for g in range(group):
            s = lax.dot_general(qh, k_refs[g][0, :, h, :], nt, preferred_element_type=F32)
            s_list.append(s + (cn[:, h:h + 1] + suffix[g][h:h + 1, :]))
        m_old = m_s[h][:, 0:1]
        m_new = m_old
        for s in s_list:
            m_new = jnp.maximum(m_new, jnp.max(s, axis=-1, keepdims=True))
        alpha = jnp.exp(m_old - m_new)
        l_new = alpha * l_s[h][:, 0:1]
        acc = alpha * acc_s[h]
        for g in range(group):
            p = jnp.exp(s_list[g] - m_new)
            l_new = l_new + jnp.sum(p, axis=-1, keepdims=True)
            acc = acc + jnp.dot(p, v_refs[g][0, :, h, :], preferred_element_type=F32)
        m_s[h] = jnp.broadcast_to(m_new, m_s.shape[1:])
        l_s[h] = jnp.broadcast_to(l_new, l_s.shape[1:])
        acc_s[h] = acc

    @pl.when(step == pl.num_programs(1) - 1)
    def _():
        for h in range(N_HEADS):
            cols = slice(h * HEAD_DIM, (h + 1) * HEAD_DIM)
            o_ref[0, :, cols] = (acc_s[h] / l_s[h][:, 0:1]).astype(o_ref.dtype)


def _fox_sample(page_table, q, k_new, v_new, c_new, c_new_t, cache_k, cache_v, page_sfx, *, group):
    b, t, d = q.shape
    n_pages = page_table.shape[1]
    page = cache_k.shape[1]
    assert n_pages % group == 0

    def page_map(g):
        return lambda bi, si, pt: (pt[bi, n_pages - 1 - (si * group + g)], 0, 0, 0)

    def sfx_map(g):
        return lambda bi, si, pt: (pt[bi, n_pages - 1 - (si * group + g)], 0, 0)

    row = pl.BlockSpec((1, t, d), lambda bi, si, pt: (bi, 0, 0))
    kv_specs = [pl.BlockSpec((1, page, N_HEADS, HEAD_DIM), page_map(g)) for g in range(group)]
    sfx_specs = [pl.BlockSpec((1, N_HEADS, page), sfx_map(g)) for g in range(group)]
    grid_spec = pltpu.PrefetchScalarGridSpec(
        num_scalar_prefetch=1,
        grid=(b, n_pages // group),
        in_specs=[row, row, row,
                  pl.BlockSpec((1, t, LANES), lambda bi, si, pt: (bi, 0, 0)),
                  pl.BlockSpec((1, N_HEADS, t), lambda bi, si, pt: (bi, 0, 0)),
                  *kv_specs, *kv_specs, *sfx_specs],
        out_specs=row,
        scratch_shapes=[pltpu.VMEM((N_HEADS, t, LANES), F32),
                        pltpu.VMEM((N_HEADS, t, LANES), F32),
                        pltpu.VMEM((N_HEADS, t, HEAD_DIM), F32),
                        pltpu.VMEM((N_HEADS, LANES), F32)],
    )
    return pl.pallas_call(
        functools.partial(_fox_sample_kernel, group=group, page=page),
        grid_spec=grid_spec,
        out_shape=jax.ShapeDtypeStruct((b, t, d), BF16),
        compiler_params=_params("arbitrary", "arbitrary"),
        name="fox_sample",
    )(page_table, q, k_new, v_new, c_new, c_new_t,
      *([cache_k] * group), *([cache_v] * group), *([page_sfx] * group))


def _trunk(x, seq_len, hist, attend, w, *, tm):
    n, d = x.shape
    gated, u_rows = _inproj_conv(x, w["g_mix_pre"][0:1], w["w_in"], w["w_conv"],
                                 tm=tm, tn=512, seq_len=seq_len, hist=hist)
    h, a = _outproj_residual(gated, w["w_out"], x, w["g_mix_post"][0:1], w["g_ffn_pre"][0:1],
                             tm=min(tm, 256))
    g_next = jnp.concatenate([w["g_kv"], w["g_mix_pre"][1:2]], axis=0)
    h, a_kv, a_q = _mlp_residual(a, w["w_up"][0], w["w_down"][0], h, w["g_ffn_post"][0:1], g_next,
                                 tm=tm, tf=512)
    k32, k16 = _proj(a_kv, w["w_kvf"], col0=0, n_out=d, out_dtypes=(F32, BF16), tm=tm, tn=512)
    v32, v16 = _proj(a_kv, w["w_kvf"], col0=d, n_out=d, out_dtypes=(F32, BF16), tm=tm, tn=512)
    lf, c = _logf(a_kv, w["w_f"], w["b_f"], tm=min(tm, 256), seq_len=seq_len)
    o = attend(a_q, k32, v32, k16, v16, c)
    h, a = _outproj_residual(o, w["w_o"], h, w["g_mix_post"][1:2], w["g_ffn_pre"][1:2],
                             tm=min(tm, 256))
    (y,) = _mlp_residual(a, w["w_up"][1], w["w_down"][1], h, w["g_ffn_post"][1:2], None,
                         tm=tm, tf=512)
    return y, u_rows, k32, v32, lf


def kernel(x_prompt, x_sample, state_conv, cache_k, cache_v, cache_logf, page_table,
           w_in_a, w_conv_a, w_out_a, g_kv, w_kvf, b_f, w_q, w_o,
           g_mix_pre, g_mix_post, g_ffn_pre, g_ffn_post, w_up, w_down):
    bp, t, d = x_prompt.shape
    bs, ts, _ = x_sample.shape
    heads = N_HEADS
    scale = HEAD_DIM ** -0.5
    w = {
        "w_in": w_in_a[0].astype(BF16), "w_conv": w_conv_a[0], "w_out": w_out_a[0].astype(BF16),
        "g_kv": g_kv[None, :], "w_kvf": w_kvf.astype(BF16),
        "w_f": jnp.pad(w_kvf[:, 2 * d:], ((0, 0), (0, LANES - heads))).astype(BF16),
        "b_f": jnp.pad(b_f, (0, LANES - heads))[None, :],
        "w_q": w_q[0].astype(BF16), "w_o": w_o[0].astype(BF16),
        "g_mix_pre": g_mix_pre, "g_mix_post": g_mix_post,
        "g_ffn_pre": g_ffn_pre, "g_ffn_post": g_ffn_post,
        "w_up": w_up.astype(BF16), "w_down": w_down.astype(BF16),
    }

    def attend_prompt(a_q, k32, v32, k16, v16, c):
        (q,) = _proj(a_q, w["w_q"], col0=0, n_out=d, out_dtypes=(BF16,), tm=512, tn=512, scale=scale)
        c3 = c.reshape(bp, t, LANES)
        c_heads = jnp.swapaxes(c3[:, :, :heads], 1, 2)
        o = _fox_prompt(q.reshape(bp, t, d), k16.reshape(bp, t, d), v16.reshape(bp, t, d),
                        c3, c_heads, tq=512, tk=512)
        return o.reshape(bp * t, d)

    y_p, u_p, k_p, v_p, lf_p = _trunk(x_prompt.reshape(bp * t, d), t, None, attend_prompt, w, tm=512)
    tiles = t // 512
    conv_prompt = u_p.reshape(bp, tiles, SUBLANES, d)[:, -1, SUBLANES - (CONV_W - 1):, :][None]

    prev = state_conv[0]
    zeros = jnp.zeros((bs, ts - 2, d), F32)
    p1 = jnp.concatenate([prev[:, 1:2], zeros, zeros[:, :1]], axis=1).reshape(bs * ts, d)
    p2 = jnp.concatenate([prev, zeros], axis=1).reshape(bs * ts, d)
    logf_t = jnp.swapaxes(cache_logf, 1, 2)
    page_sfx = _page_suffix(logf_t, pages_per_step=math.gcd(logf_t.shape[0], 64))

    def attend_sample(a_q, k32, v32, k16, v16, c):
        (q,) = _proj(a_q, w["w_q"], col0=0, n_out=d, out_dtypes=(F32,), tm=bs * ts, tn=512, scale=scale)
        c3 = c.reshape(bs, ts, LANES)
        c_t = jnp.swapaxes(c3[:, :, :heads], 1, 2)
        o = _fox_sample(page_table, q.reshape(bs, ts, d), k32.reshape(bs, ts, d),
                        v32.reshape(bs, ts, d), c3, c_t, cache_k, cache_v, page_sfx, group=4)
        return o.reshape(bs * ts, d)

    y_s, u_s, k_s, v_s, lf_s = _trunk(x_sample.reshape(bs * ts, d), ts, (p1, p2), attend_sample, w,
                                      tm=bs * ts)
    conv_sample = u_s.reshape(bs, ts, d)[:, ts - (CONV_W - 1):, :][None]

    return (y_p.reshape(bp, t, d), y_s.reshape(bs, ts, d), conv_prompt, conv_sample,
            k_p.reshape(bp, t, heads, HEAD_DIM), v_p.reshape(bp, t, heads, HEAD_DIM),
            lf_p[:, :heads].reshape(bp, t, heads),
            k_s.reshape(bs, ts, heads, HEAD_DIM), v_s.reshape(bs, ts, heads, HEAD_DIM),
            lf_s[:, :heads].reshape(bs, ts, heads))
```

```python
import functools
import math

import jax
import jax.numpy as jnp
from jax import lax
from jax.experimental import pallas as pl
from jax.experimental.pallas import tpu as pltpu

RMS_EPS = 1e-6
CONV_W = 3
N_HEADS = 16
HEAD_DIM = 128
LANES = 128
SUBLANES = 8
VMEM_LIMIT = 56 * 1024 * 1024
NEG_BIG = -1e30
LOG2E = math.log2(math.e)

F32 = jnp.float32
BF16 = jnp.bfloat16


def _params(*sem):
    return pltpu.CompilerParams(dimension_semantics=sem, vmem_limit_bytes=VMEM_LIMIT)


def _rms(x, g):
    return x * lax.rsqrt(jnp.mean(x * x, axis=-1, keepdims=True) + RMS_EPS) * g


def _split3(x):
    a = x.astype(BF16)
    r = x - a.astype(F32)
    b = r.astype(BF16)
    c = (r - b.astype(F32)).astype(BF16)
    return a, b, c


def _dot01_exact(m01, x):
    a, b, c = _split3(x)
    d = functools.partial(jnp.dot, preferred_element_type=F32)
    return d(m01, a) + d(m01, b) + d(m01, c)


def _inproj_conv_kernel(*refs, tm, tiles_per_seq, seq_len, inject):
    if inject:
        (x_ref, g_ref, wb_ref, wc_ref, wx_ref, wconv_ref, p1_ref, p2_ref,
         gated_ref, u_ref, a_s) = refs
    else:
        (x_ref, g_ref, wb_ref, wc_ref, wx_ref, wconv_ref,
         gated_ref, u_ref, a_s, carry_s) = refs
    i = pl.program_id(0)
    j = pl.program_id(1)

    @pl.when(j == 0)
    def _():
        a_s[...] = _rms(x_ref[...], g_ref[...]).astype(BF16)

    a = a_s[...]
    dot = functools.partial(jnp.dot, preferred_element_type=F32)
    u = dot(a, wc_ref[...]) * dot(a, wx_ref[...])
    rolled1 = pltpu.roll(u, 1, axis=0)
    rolled2 = pltpu.roll(u, 2, axis=0)
    row = lax.broadcasted_iota(jnp.int32, u.shape, 0)
    if inject:
        pos = row % seq_len
        u1 = jnp.where(pos >= 1, rolled1, p1_ref[...])
        u2 = jnp.where(pos >= 2, rolled2, p2_ref[...])
        u_ref[...] = u
    else:
        @pl.when(i % tiles_per_seq == 0)
        def _():
            carry_s[j] = jnp.zeros(carry_s.shape[1:], F32)

        tail = carry_s[j]
        t1 = tail[SUBLANES - 1:SUBLANES, :]
        t2 = tail[SUBLANES - 2:SUBLANES - 1, :]
        u1 = jnp.where(row == 0, t1, rolled1)
        u2 = jnp.where(row == 0, t2, jnp.where(row == 1, t1, rolled2))
        last = u[tm - SUBLANES:, :]
        carry_s[j] = last
        u_ref[...] = last
    wconv = wconv_ref[...]
    conv = wconv[0:1, :] * u2 + wconv[1:2, :] * u1 + wconv[2:3, :] * u
    gated_ref[...] = (dot(a, wb_ref[...]) * conv).astype(BF16)


def _inproj_conv(x, g, w_in, w_conv, *, tm, tn, seq_len, hist=None):
    n, d = x.shape
    nj = d // tn
    inject = hist is not None
    assert n % tm == 0 and d % tn == 0
    if not inject:
        assert seq_len % tm == 0
    x_spec = pl.BlockSpec((tm, d), lambda i, j: (i, 0))
    g_spec = pl.BlockSpec((1, d), lambda i, j: (0, 0))
    w_specs = [pl.BlockSpec((d, tn), lambda i, j, s=s: (0, j + s * nj)) for s in range(3)]
    wconv_spec = pl.BlockSpec((CONV_W, tn), lambda i, j: (0, j))
    tile_spec = pl.BlockSpec((tm, tn), lambda i, j: (i, j))
    in_specs = [x_spec, g_spec, *w_specs, wconv_spec]
    args = [x, g, w_in, w_in, w_in, w_conv]
    scratch = [pltpu.VMEM((tm, d), BF16)]
    if inject:
        in_specs += [tile_spec, tile_spec]
        args += list(hist)
        u_shape = jax.ShapeDtypeStruct((n, d), F32)
        u_spec = tile_spec
    else:
        scratch.append(pltpu.VMEM((nj, SUBLANES, tn), F32))
        u_shape = jax.ShapeDtypeStruct((n // tm * SUBLANES, d), F32)
        u_spec = pl.BlockSpec((SUBLANES, tn), lambda i, j: (i, j))
    kern = functools.partial(_inproj_conv_kernel, tm=tm, tiles_per_seq=max(seq_len // tm, 1),
                             seq_len=seq_len, inject=inject)
    return pl.pallas_call(
        kern,
        grid=(n // tm, nj),
        in_specs=in_specs,
        out_specs=[tile_spec, u_spec],
        out_shape=[jax.ShapeDtypeStruct((n, d), BF16), u_shape],
        scratch_shapes=scratch,
        compiler_params=_params("arbitrary", "arbitrary"),
        name="inproj_conv",
    )(*args)


def _outproj_kernel(a_ref, w_ref, h_ref, gpost_ref, gnext_ref, h1_ref, anext_ref):
    mix = jnp.dot(a_ref[...], w_ref[...], preferred_element_type=F32)
    h1 = h_ref[...] + _rms(mix, gpost_ref[...])
    h1_ref[...] = h1
    anext_ref[...] = _rms(h1, gnext_ref[...]).astype(BF16)


def _outproj_residual(a, w, h, g_post, g_next, *, tm):
    n, d = h.shape
    row = pl.BlockSpec((tm, d), lambda i: (i, 0))
    vec = pl.BlockSpec((1, d), lambda i: (0, 0))
    return pl.pallas_call(
        _outproj_kernel,
        grid=(n // tm,),
        in_specs=[row, pl.BlockSpec((d, d), lambda i: (0, 0)), row, vec, vec],
        out_specs=[row, row],
        out_shape=[jax.ShapeDtypeStruct((n, d), F32), jax.ShapeDtypeStruct((n, d), BF16)],
        compiler_params=_params("arbitrary"),
        name="outproj_residual",
    )(a, w, h, g_post, g_next)


def _mlp_kernel(*refs, n_next):
    a_ref, wup_ref, wdown_ref, h_ref, gpost_ref, gnext_ref = refs[:6]
    h2_ref = refs[6]
    next_refs = refs[7:7 + n_next]
    acc_s = refs[7 + n_next]
    f = pl.program_id(1)

    @pl.when(f == 0)
    def _():
        acc_s[...] = jnp.zeros(acc_s.shape, F32)

    hid = jnp.dot(a_ref[...], wup_ref[...], preferred_element_type=F32)
    hid = jnp.square(jnp.maximum(hid, 0.0)).astype(BF16)
    acc_s[...] += jnp.dot(hid, wdown_ref[...], preferred_element_type=F32)

    @pl.when(f == pl.num_programs(1) - 1)
    def _():
        h2 = h_ref[...] + _rms(acc_s[...], gpost_ref[...])
        h2_ref[...] = h2
        gnext = gnext_ref[...]
        for k, ref in enumerate(next_refs):
            ref[...] = _rms(h2, gnext[k:k + 1, :]).astype(BF16)


def _mlp_residual(a, w_up, w_down, h, g_post, g_next, *, tm, tf):
    n, d = h.shape
    d_ff = w_up.shape[1]
    n_next = 0 if g_next is None else g_next.shape[0]
    g_arr = jnp.ones((1, d), F32) if g_next is None else g_next
    row = pl.BlockSpec((tm, d), lambda i, f: (i, 0))
    outs = [jax.ShapeDtypeStruct((n, d), F32)] + [jax.ShapeDtypeStruct((n, d), BF16)] * n_next
    return pl.pallas_call(
        functools.partial(_mlp_kernel, n_next=n_next),
        grid=(n // tm, d_ff // tf),
        in_specs=[row,
                  pl.BlockSpec((d, tf), lambda i, f: (0, f)),
                  pl.BlockSpec((tf, d), lambda i, f: (f, 0)),
                  row,
                  pl.BlockSpec((1, d), lambda i, f: (0, 0)),
                  pl.BlockSpec(g_arr.shape, lambda i, f: (0, 0))],
        out_specs=[row] * (1 + n_next),
        out_shape=outs,
        scratch_shapes=[pltpu.VMEM((tm, d), F32)],
        compiler_params=_params("arbitrary", "arbitrary"),
        name="mlp_residual",
    )(a, w_up, w_down, h, g_post, g_arr)


def _proj_kernel(a_ref, w_ref, *out_refs, scale):
    y = jnp.dot(a_ref[...], w_ref[...], preferred_element_type=F32)
    if scale is not None:
        y = y * scale
    for ref in out_refs:
        ref[...] = y.astype(ref.dtype)


def _proj(a, w, *, col0, n_out, out_dtypes, tm, tn, scale=None):
    n, d = a.shape
    assert col0 % tn == 0 and n_out % tn == 0
    jb = col0 // tn
    tile = pl.BlockSpec((tm, tn), lambda i, j: (i, j))
    return pl.pallas_call(
        functools.partial(_proj_kernel, scale=scale),
        grid=(n // tm, n_out // tn),
        in_specs=[pl.BlockSpec((tm, d), lambda i, j: (i, 0)),
                  pl.BlockSpec((d, tn), lambda i, j: (0, j + jb))],
        out_specs=[tile] * len(out_dtypes),
        out_shape=[jax.ShapeDtypeStruct((n, n_out), dt) for dt in out_dtypes],
        compiler_params=_params("arbitrary", "arbitrary"),
        name="proj",
    )(a, w)


def _logf_kernel(a_ref, wf_ref, bf_ref, lf_ref, c_ref, carry_s, *, tm, tiles_per_seq, seq_len):
    i = pl.program_id(0)
    x = jnp.dot(a_ref[...], wf_ref[...], preferred_element_type=F32) + bf_ref[...]
    lf = jnp.minimum(x, 0.0) - jnp.log1p(jnp.exp(-jnp.abs(x)))
    lf_ref[...] = lf
    r = lax.broadcasted_iota(jnp.int32, (tm, tm), 0)
    c = lax.broadcasted_iota(jnp.int32, (tm, tm), 1)
    same_seq = (r // seq_len) == (c // seq_len) if seq_len < tm else (r >= 0)
    tri = jnp.where((c <= r) & same_seq, 1.0, 0.0).astype(BF16)
    csum = _dot01_exact(tri, lf)

    @pl.when(i % tiles_per_seq == 0)
    def _():
        carry_s[...] = jnp.zeros(carry_s.shape, F32)

    csum = csum + carry_s[0:1, :]
    c_ref[...] = csum
    carry_s[...] = jnp.broadcast_to(csum[tm - 1:tm, :], carry_s.shape)


def _logf(a, w_f, b_f, *, tm, seq_len):
    n, d = a.shape
    assert seq_len % tm == 0 or tm % seq_len == 0
    tile = pl.BlockSpec((tm, LANES), lambda i: (i, 0))
    kern = functools.partial(_logf_kernel, tm=tm, tiles_per_seq=max(seq_len // tm, 1), seq_len=seq_len)
    return pl.pallas_call(
        kern,
        grid=(n // tm,),
        in_specs=[pl.BlockSpec((tm, d), lambda i: (i, 0)),
                  pl.BlockSpec((d, LANES), lambda i: (0, 0)),
                  pl.BlockSpec((1, LANES), lambda i: (0, 0))],
        out_specs=[tile, tile],
        out_shape=[jax.ShapeDtypeStruct((n, LANES), F32)] * 2,
        scratch_shapes=[pltpu.VMEM((SUBLANES, LANES), F32)],
        compiler_params=_params("arbitrary"),
        name="logf",
    )(a, w_f, b_f)


def _online_softmax_update(chunks, cq, m_old, l_old):
    mx = functools.reduce(jnp.maximum, chunks)
    rowmax = jnp.broadcast_to(jnp.max(mx, axis=-1, keepdims=True), mx.shape)
    m_new = jnp.maximum(m_old, rowmax + cq)
    alpha = jnp.exp2(m_old - m_new)
    shift = m_new - cq
    p = [jnp.exp2(c - shift) for c in chunks]
    rowsum = jnp.broadcast_to(jnp.sum(functools.reduce(jnp.add, p), axis=-1, keepdims=True), mx.shape)
    return m_new, alpha * l_old + rowsum, alpha, p


def _fox_prompt_kernel(q_ref, k_ref, v_ref, cq_ref, ck_ref, o_ref, m_s, l_s, acc_s, cqb_s, *, tq, tk):
    qi = pl.program_id(1)
    ki = pl.program_id(2)
    n_chunks = tk // LANES

    @pl.when(ki == 0)
    def _():
        m_s[...] = jnp.full(m_s.shape, -jnp.inf, F32)
        l_s[...] = jnp.zeros(l_s.shape, F32)
        acc_s[...] = jnp.zeros(acc_s.shape, F32)
        cq = cq_ref[0]
        for h in range(N_HEADS):
            cqb_s[h] = jnp.broadcast_to(cq[:, h:h + 1], cqb_s.shape[1:])

    def step(on_diagonal):
        ck = ck_ref[0]
        if on_diagonal:
            row = lax.broadcasted_iota(jnp.int32, (tq, LANES), 0)
            lane = lax.broadcasted_iota(jnp.int32, (tq, LANES), 1)
        for h in range(N_HEADS):
            cols = slice(h * HEAD_DIM, (h + 1) * HEAD_DIM)
            s = lax.dot_general(q_ref[0, :, cols], k_ref[0, :, cols],
                                (((1,), (1,)), ((), ())), preferred_element_type=F32)
            chunks = []
            for c in range(n_chunks):
                lanes = slice(c * LANES, (c + 1) * LANES)
                sc = s[:, lanes] - ck[h:h + 1, lanes]
                if on_diagonal:
                    sc = jnp.where(lane + c * LANES <= row, sc, NEG_BIG)
                chunks.append(sc)
            m_new, l_new, alpha, p = _online_softmax_update(chunks, cqb_s[h], m_s[h], l_s[h])
            m_s[h] = m_new
            l_s[h] = l_new
            pv = jnp.dot(jnp.concatenate(p, axis=1).astype(BF16), v_ref[0, :, cols],
                         preferred_element_type=F32)
            acc_s[:, cols] = alpha * acc_s[:, cols] + pv

    pl.when(ki < qi)(functools.partial(step, False))
    pl.when(ki == qi)(functools.partial(step, True))

    @pl.when(ki == pl.num_programs(2) - 1)
    def _():
        for h in range(N_HEADS):
            cols = slice(h * HEAD_DIM, (h + 1) * HEAD_DIM)
            o_ref[0, :, cols] = (acc_s[:, cols] / l_s[h]).astype(o_ref.dtype)


def _fox_prompt(q, k, v, c_rows, c_heads, *, tq, tk):
    b, t, d = q.shape
    assert tq == tk
    kv_map = lambda bi, qi, ki: (bi, jnp.minimum(ki, qi), 0)
    return pl.pallas_call(
        functools.partial(_fox_prompt_kernel, tq=tq, tk=tk),
        grid=(b, t // tq, t // tk),
        in_specs=[pl.BlockSpec((1, tq, d), lambda bi, qi, ki: (bi, qi, 0)),
                  pl.BlockSpec((1, tk, d), kv_map),
                  pl.BlockSpec((1, tk, d), kv_map),
                  pl.BlockSpec((1, tq, LANES), lambda bi, qi, ki: (bi, qi, 0)),
                  pl.BlockSpec((1, N_HEADS, tk), lambda bi, qi, ki: (bi, 0, jnp.minimum(ki, qi)))],
        out_specs=pl.BlockSpec((1, tq, d), lambda bi, qi, ki: (bi, qi, 0)),
        out_shape=jax.ShapeDtypeStruct((b, t, d), BF16),
        scratch_shapes=[pltpu.VMEM((N_HEADS, tq, LANES), F32),
                        pltpu.VMEM((N_HEADS, tq, LANES), F32),
                        pltpu.VMEM((tq, d), F32),
                        pltpu.VMEM((N_HEADS, tq, LANES), F32)],
        compiler_params=_params("arbitrary", "arbitrary", "arbitrary"),
        name="fox_prompt",
    )(q, k, v, c_rows, c_heads)


def _page_suffix_kernel(lf_ref, out_ref, *, heads):
    x = lf_ref[...]
    width = x.shape[1]
    lane = lax.broadcasted_iota(jnp.int32, x.shape, 1)
    incl = x
    k = heads
    while k < width:
        incl = incl + jnp.where(lane < width - k, pltpu.roll(incl, width - k, axis=1), 0.0)
        k *= 2
    total = jnp.where(lane < heads, incl, 0.0)
    k = heads
    while k < width:
        total = total + pltpu.roll(total, k, axis=1)
        k *= 2
    out_ref[:, 0, :] = (incl - x) * LOG2E
    out_ref[:, 1, :] = total * LOG2E


def _page_suffix(logf_flat, *, heads, pages_per_step):
    n_pool, width = logf_flat.shape
    assert n_pool % pages_per_step == 0
    return pl.pallas_call(
        functools.partial(_page_suffix_kernel, heads=heads),
        grid=(n_pool // pages_per_step,),
        in_specs=[pl.BlockSpec((pages_per_step, width), lambda i: (i, 0))],
        out_specs=pl.BlockSpec((pages_per_step, 2, width), lambda i: (i, 0, 0)),
        out_shape=jax.ShapeDtypeStruct((n_pool, 2, width), F32),
        compiler_params=_params("arbitrary"),
        name="page_suffix",
    )(logf_flat)


def _fox_sample_kernel(pt_ref, q_ref, kn_ref, vn_ref, cq_ref, cnrow_ref, *refs, group, t_new):
    k_refs = refs[:group]
    v_refs = refs[group:2 * group]
    s_refs = refs[2 * group:3 * group]
    o_ref = refs[3 * group]
    m_s, l_s, acc_s, cqb_s, mask_s, carry_s = refs[3 * group + 1:]
    step = pl.program_id(1)
    rows = q_ref.shape[1]
    heads = rows // t_new
    width = mask_s.shape[1]
    nt = (((1,), (1,)), ((), ()))
    q = q_ref[0]

    def update(s, v, mask_chunks, bias_row):
        n_chunks = s.shape[1] // LANES
        chunks = []
        for c in range(n_chunks):
            lanes = slice(c * LANES, (c + 1) * LANES)
            chunks.append(s[:, lanes] + mask_chunks(c) + bias_row[:, lanes])
        m_new, l_new, alpha, p = _online_softmax_update(chunks, cqb_s[...], m_s[...], l_s[...])
        m_s[...] = m_new
        l_s[...] = l_new
        acc_s[...] = alpha * acc_s[...] + jnp.dot(jnp.concatenate(p, axis=1), v,
                                                  preferred_element_type=F32)

    @pl.when(step == 0)
    def _():
        m_s[...] = jnp.full(m_s.shape, -jnp.inf, F32)
        l_s[...] = jnp.zeros(l_s.shape, F32)
        acc_s[...] = jnp.zeros(acc_s.shape, F32)
        carry_s[...] = jnp.zeros(carry_s.shape, F32)
        cqb_s[...] = jnp.broadcast_to(cq_ref[0], cqb_s.shape)
        row_head = lax.broadcasted_iota(jnp.int32, (rows, width), 0) // t_new
        col_head = lax.broadcasted_iota(jnp.int32, (rows, width), 1) % heads
        mask_s[...] = jnp.where(row_head == col_head, 0.0, NEG_BIG)
        r = lax.broadcasted_iota(jnp.int32, (rows, LANES), 0)
        c = lax.broadcasted_iota(jnp.int32, (rows, LANES), 1)
        new_mask = jnp.where((r // t_new == c % heads) & (c // heads <= r % t_new), 0.0, NEG_BIG)
        s = lax.dot_general(q, kn_ref[0], nt, preferred_element_type=F32)
        update(s, vn_ref[0], lambda _: new_mask, -cnrow_ref[0])

    for g in range(group):
        sfx = s_refs[g][0]
        bias_row = carry_s[...] + sfx[0:1, :]
        carry_s[...] = carry_s[...] + sfx[1:2, :]
        kf = k_refs[g][0].reshape(width, HEAD_DIM)
        vf = v_refs[g][0].reshape(width, HEAD_DIM)
        s = lax.dot_general(q, kf, nt, preferred_element_type=F32)
        update(s, vf, lambda c: mask_s[:, c * LANES:(c + 1) * LANES], bias_row)

    @pl.when(step == pl.num_programs(1) - 1)
    def _():
        o_ref[0] = (acc_s[...] / l_s[...]).astype(o_ref.dtype)


def _fox_sample(page_table, q, k_new, v_new, cq_col, cn_row, cache_k, cache_v, page_sfx, *, group, t_new):
    b, rows, dh = q.shape
    n_pages = page_table.shape[1]
    _, page, heads, _ = cache_k.shape
    width = page * heads
    assert n_pages % group == 0 and rows == heads * t_new and rows == LANES

    def page_map(g, ndim):
        return lambda bi, si, pt: (pt[bi, n_pages - 1 - (si * group + g)],) + (0,) * (ndim - 1)

    per_seq = lambda shape: pl.BlockSpec((1,) + shape, lambda bi, si, pt: (bi, 0, 0))
    kv_specs = [pl.BlockSpec((1, page, heads, dh), page_map(g, 4)) for g in range(group)]
    sfx_specs = [pl.BlockSpec((1, 2, width), page_map(g, 3)) for g in range(group)]
    stat = pltpu.VMEM((rows, LANES), F32)
    grid_spec = pltpu.PrefetchScalarGridSpec(
        num_scalar_prefetch=1,
        grid=(b, n_pages // group),
        in_specs=[per_seq((rows, dh)), per_seq((rows, dh)), per_seq((rows, dh)),
                  per_seq((rows, 1)), per_seq((1, rows)),
                  *kv_specs, *kv_specs, *sfx_specs],
        out_specs=per_seq((rows, dh)),
        scratch_shapes=[stat, stat, pltpu.VMEM((rows, dh), F32), stat,
                        pltpu.VMEM((rows, width), F32), pltpu.VMEM((1, width), F32)],
    )
    return pl.pallas_call(
        functools.partial(_fox_sample_kernel, group=group, t_new=t_new),
        grid_spec=grid_spec,
        out_shape=jax.ShapeDtypeStruct((b, rows, dh), BF16),
        compiler_params=_params("arbitrary", "arbitrary"),
        name="fox_sample",
    )(page_table, q, k_new, v_new, cq_col, cn_row,
      *([cache_k] * group), *([cache_v] * group), *([page_sfx] * group))


def _trunk(x, seq_len, hist, attend, q_dtype, w, *, tm):
    n, d = x.shape
    gated, u_rows = _inproj_conv(x, w["g_mix_pre"][0:1], w["w_in"], w["w_conv"],
                                 tm=tm, tn=512, seq_len=seq_len, hist=hist)
    h, a = _outproj_residual(gated, w["w_out"], x, w["g_mix_post"][0:1], w["g_ffn_pre"][0:1],
                             tm=min(tm, 256))
    g_next = jnp.concatenate([w["g_kv"], w["g_mix_pre"][1:2]], axis=0)
    h, a_kv, a_q = _mlp_residual(a, w["w_up"][0], w["w_down"][0], h, w["g_ffn_post"][0:1], g_next,
                                 tm=tm, tf=512)
    ptm = min(n, 1024)
    k32, k16 = _proj(a_kv, w["w_kvf"], col0=0, n_out=d, out_dtypes=(F32, BF16), tm=ptm, tn=1024)
    v32, v16 = _proj(a_kv, w["w_kvf"], col0=d, n_out=d, out_dtypes=(F32, BF16), tm=ptm, tn=1024)
    lf, c = _logf(a_kv, w["w_f"], w["b_f"], tm=min(tm, 256), seq_len=seq_len)
    (q,) = _proj(a_q, w["w_q"], col0=0, n_out=d, out_dtypes=(q_dtype,), tm=ptm, tn=1024,
                 scale=LOG2E * HEAD_DIM ** -0.5)
    o = attend(q, k32, v32, k16, v16, c * LOG2E)
    h, a = _outproj_residual(o, w["w_o"], h, w["g_mix_post"][1:2], w["g_ffn_pre"][1:2],
                             tm=min(tm, 256))
    (y,) = _mlp_residual(a, w["w_up"][1], w["w_down"][1], h, w["g_ffn_post"][1:2], None,
                         tm=tm, tf=512)
    return y, u_rows, k32, v32, lf


def kernel(x_prompt, x_sample, state_conv, cache_k, cache_v, cache_logf, page_table,
           w_in_a, w_conv_a, w_out_a, g_kv, w_kvf, b_f, w_q, w_o,
           g_mix_pre, g_mix_post, g_ffn_pre, g_ffn_post, w_up, w_down):
    bp, t, d = x_prompt.shape
    bs, ts, _ = x_sample.shape
    heads = N_HEADS
    w = {
        "w_in": w_in_a[0].astype(BF16), "w_conv": w_conv_a[0], "w_out": w_out_a[0].astype(BF16),
        "g_kv": g_kv[None, :], "w_kvf": w_kvf.astype(BF16),
        "w_f": jnp.pad(w_kvf[:, 2 * d:], ((0, 0), (0, LANES - heads))).astype(BF16),
        "b_f": jnp.pad(b_f, (0, LANES - heads))[None, :],
        "w_q": w_q[0].astype(BF16), "w_o": w_o[0].astype(BF16),
        "g_mix_pre": g_mix_pre, "g_mix_post": g_mix_post,
        "g_ffn_pre": g_ffn_pre, "g_ffn_post": g_ffn_post,
        "w_up": w_up.astype(BF16), "w_down": w_down.astype(BF16),
    }

    def attend_prompt(q, k32, v32, k16, v16, c):
        c3 = c.reshape(bp, t, LANES)
        c_heads = jnp.swapaxes(c3[:, :, :heads], 1, 2)
        o = _fox_prompt(q.reshape(bp, t, d), k16.reshape(bp, t, d), v16.reshape(bp, t, d),
                        c3, c_heads, tq=512, tk=512)
        return o.reshape(bp * t, d)

    y_p, u_p, k_p, v_p, lf_p = _trunk(x_prompt.reshape(bp * t, d), t, None, attend_prompt, BF16, w,
                                      tm=512)
    tiles = t // 512
    conv_prompt = u_p.reshape(bp, tiles, SUBLANES, d)[:, -1, SUBLANES - (CONV_W - 1):, :][None]

    prev = state_conv[0]
    zeros = jnp.zeros((bs, ts - 2, d), F32)
    p1 = jnp.concatenate([prev[:, 1:2], zeros, zeros[:, :1]], axis=1).reshape(bs * ts, d)
    p2 = jnp.concatenate([prev, zeros], axis=1).reshape(bs * ts, d)
    n_pool, page = cache_logf.shape[:2]
    page_sfx = _page_suffix(cache_logf.reshape(n_pool, page * heads), heads=heads,
                            pages_per_step=64 if n_pool % 64 == 0 else n_pool)

    def attend_sample(q, k32, v32, k16, v16, c):
        q_rows = q.reshape(bs, ts, heads, HEAD_DIM).transpose(0, 2, 1, 3).reshape(bs, heads * ts, HEAD_DIM)
        cn = c[:, :heads].reshape(bs, ts, heads)
        o = _fox_sample(page_table, q_rows,
                        k32.reshape(bs, ts * heads, HEAD_DIM), v32.reshape(bs, ts * heads, HEAD_DIM),
                        cn.transpose(0, 2, 1).reshape(bs, heads * ts, 1), cn.reshape(bs, 1, ts * heads),
                        cache_k, cache_v, page_sfx, group=8, t_new=ts)
        return o.reshape(bs, heads, ts, HEAD_DIM).transpose(0, 2, 1, 3).reshape(bs * ts, d)

    y_s, u_s, k_s, v_s, lf_s = _trunk(x_sample.reshape(bs * ts, d), ts, (p1, p2), attend_sample, F32, w,
                                      tm=bs * ts)
    conv_sample = u_s.reshape(bs, ts, d)[:, ts - (CONV_W - 1):, :][None]

    return (y_p.reshape(bp, t, d), y_s.reshape(bs, ts, d), conv_prompt, conv_sample,
            k_p.reshape(bp, t, heads, HEAD_DIM), v_p.reshape(bp, t, heads, HEAD_DIM),
            lf_p[:, :heads].reshape(bp, t, heads),
            k_s.reshape(bs, ts, heads, HEAD_DIM), v_s.reshape(bs, ts, heads, HEAD_DIM),
            lf_s[:, :heads].reshape(bs, ts, heads))
```

```python
import functools
import math

import jax
import jax.numpy as jnp
from jax import lax
from jax.experimental import pallas as pl
from jax.experimental.pallas import tpu as pltpu

RMS_EPS = 1e-6
CONV_W = 3
N_HEADS = 16
HEAD_DIM = 128
LANES = 128
SUBLANES = 8
VMEM_LIMIT = 56 * 1024 * 1024
NEG_BIG = -1e30
LOG2E = math.log2(math.e)
SAMPLE_JOINT = 4

F32 = jnp.float32
BF16 = jnp.bfloat16


def _params(*sem):
    return pltpu.CompilerParams(dimension_semantics=sem, vmem_limit_bytes=VMEM_LIMIT)


def _rms(x, g):
    return x * lax.rsqrt(jnp.mean(x * x, axis=-1, keepdims=True) + RMS_EPS) * g


def _split3(x):
    a = x.astype(BF16)
    r = x - a.astype(F32)
    b = r.astype(BF16)
    c = (r - b.astype(F32)).astype(BF16)
    return a, b, c


def _dot01_exact(m01, x):
    a, b, c = _split3(x)
    d = functools.partial(jnp.dot, preferred_element_type=F32)
    return d(m01, a) + d(m01, b) + d(m01, c)


def _inproj_conv_kernel(*refs, tm, tiles_per_seq, seq_len, inject):
    if inject:
        (x_ref, g_ref, wb_ref, wc_ref, wx_ref, wconv_ref, p1_ref, p2_ref,
         gated_ref, u_ref, a_s) = refs
    else:
        (x_ref, g_ref, wb_ref, wc_ref, wx_ref, wconv_ref,
         gated_ref, u_ref, a_s, carry_s) = refs
    i = pl.program_id(0)
    j = pl.program_id(1)

    @pl.when(j == 0)
    def _():
        a_s[...] = _rms(x_ref[...], g_ref[...]).astype(BF16)

    a = a_s[...]
    dot = functools.partial(jnp.dot, preferred_element_type=F32)
    u = dot(a, wc_ref[...]) * dot(a, wx_ref[...])
    rolled1 = pltpu.roll(u, 1, axis=0)
    rolled2 = pltpu.roll(u, 2, axis=0)
    row = lax.broadcasted_iota(jnp.int32, u.shape, 0)
    if inject:
        pos = row % seq_len
        u1 = jnp.where(pos >= 1, rolled1, p1_ref[...])
        u2 = jnp.where(pos >= 2, rolled2, p2_ref[...])
        u_ref[...] = u
    else:
        @pl.when(i % tiles_per_seq == 0)
        def _():
            carry_s[j] = jnp.zeros(carry_s.shape[1:], F32)

        tail = carry_s[j]
        t1 = tail[SUBLANES - 1:SUBLANES, :]
        t2 = tail[SUBLANES - 2:SUBLANES - 1, :]
        u1 = jnp.where(row == 0, t1, rolled1)
        u2 = jnp.where(row == 0, t2, jnp.where(row == 1, t1, rolled2))
        last = u[tm - SUBLANES:, :]
        carry_s[j] = last
        u_ref[...] = last
    wconv = wconv_ref[...]
    conv = wconv[0:1, :] * u2 + wconv[1:2, :] * u1 + wconv[2:3, :] * u
    gated_ref[...] = (dot(a, wb_ref[...]) * conv).astype(BF16)


def _inproj_conv(x, g, w_in, w_conv, *, tm, tn, seq_len, hist=None):
    n, d = x.shape
    nj = d // tn
    inject = hist is not None
    assert n % tm == 0 and d % tn == 0
    if not inject:
        assert seq_len % tm == 0
    x_spec = pl.BlockSpec((tm, d), lambda i, j: (i, 0))
    g_spec = pl.BlockSpec((1, d), lambda i, j: (0, 0))
    w_specs = [pl.BlockSpec((d, tn), lambda i, j, s=s: (0, j + s * nj)) for s in range(3)]
    wconv_spec = pl.BlockSpec((CONV_W, tn), lambda i, j: (0, j))
    tile_spec = pl.BlockSpec((tm, tn), lambda i, j: (i, j))
    in_specs = [x_spec, g_spec, *w_specs, wconv_spec]
    args = [x, g, w_in, w_in, w_in, w_conv]
    scratch = [pltpu.VMEM((tm, d), BF16)]
    if inject:
        in_specs += [tile_spec, tile_spec]
        args += list(hist)
        u_shape = jax.ShapeDtypeStruct((n, d), F32)
        u_spec = tile_spec
    else:
        scratch.append(pltpu.VMEM((nj, SUBLANES, tn), F32))
        u_shape = jax.ShapeDtypeStruct((n // tm * SUBLANES, d), F32)
        u_spec = pl.BlockSpec((SUBLANES, tn), lambda i, j: (i, j))
    kern = functools.partial(_inproj_conv_kernel, tm=tm, tiles_per_seq=max(seq_len // tm, 1),
                             seq_len=seq_len, inject=inject)
    return pl.pallas_call(
        kern,
        grid=(n // tm, nj),
        in_specs=in_specs,
        out_specs=[tile_spec, u_spec],
        out_shape=[jax.ShapeDtypeStruct((n, d), BF16), u_shape],
        scratch_shapes=scratch,
        compiler_params=_params("arbitrary", "arbitrary"),
        name="inproj_conv",
    )(*args)


def _outproj_kernel(a_ref, w_ref, h_ref, gpost_ref, gnext_ref, h1_ref, anext_ref):
    mix = jnp.dot(a_ref[...], w_ref[...], preferred_element_type=F32)
    h1 = h_ref[...] + _rms(mix, gpost_ref[...])
    h1_ref[...] = h1
    anext_ref[...] = _rms(h1, gnext_ref[...]).astype(BF16)


def _outproj_residual(a, w, h, g_post, g_next, *, tm):
    n, d = h.shape
    row = pl.BlockSpec((tm, d), lambda i: (i, 0))
    vec = pl.BlockSpec((1, d), lambda i: (0, 0))
    return pl.pallas_call(
        _outproj_kernel,
        grid=(n // tm,),
        in_specs=[row, pl.BlockSpec((d, d), lambda i: (0, 0)), row, vec, vec],
        out_specs=[row, row],
        out_shape=[jax.ShapeDtypeStruct((n, d), F32), jax.ShapeDtypeStruct((n, d), BF16)],
        compiler_params=_params("arbitrary"),
        name="outproj_residual",
    )(a, w, h, g_post, g_next)


def _mlp_step(f, n_f, a_ref, wup_ref, wdown_ref, h_ref, gpost_ref, gnext_ref, h2_ref, next_refs, acc_s):
    @pl.when(f == 0)
    def _():
        acc_s[...] = jnp.zeros(acc_s.shape, F32)

    hid = jnp.dot(a_ref[...], wup_ref[...], preferred_element_type=F32)
    hid = jnp.square(jnp.maximum(hid, 0.0)).astype(BF16)
    acc_s[...] += jnp.dot(hid, wdown_ref[...], preferred_element_type=F32)

    @pl.when(f == n_f - 1)
    def _():
        h2 = h_ref[...] + _rms(acc_s[...], gpost_ref[...])
        h2_ref[...] = h2
        for k, ref in enumerate(next_refs):
            ref[...] = _rms(h2, gnext_ref[k:k + 1, :]).astype(BF16)


def _mlp_kernel(*refs, n_next):
    a_ref, wup_ref, wdown_ref, h_ref, gpost_ref, gnext_ref = refs[:6]
    _mlp_step(pl.program_id(1), pl.num_programs(1), a_ref, wup_ref, wdown_ref, h_ref, gpost_ref,
              gnext_ref, refs[6], refs[7:7 + n_next], refs[7 + n_next])


def _mlp_residual(a, w_up, w_down, h, g_post, g_next, *, tm, tf):
    n, d = h.shape
    d_ff = w_up.shape[1]
    n_next = 0 if g_next is None else g_next.shape[0]
    g_arr = jnp.ones((1, d), F32) if g_next is None else g_next
    row = pl.BlockSpec((tm, d), lambda i, f: (i, 0))
    outs = [jax.ShapeDtypeStruct((n, d), F32)] + [jax.ShapeDtypeStruct((n, d), BF16)] * n_next
    return pl.pallas_call(
        functools.partial(_mlp_kernel, n_next=n_next),
        grid=(n // tm, d_ff // tf),
        in_specs=[row,
                  pl.BlockSpec((d, tf), lambda i, f: (0, f)),
                  pl.BlockSpec((tf, d), lambda i, f: (f, 0)),
                  row,
                  pl.BlockSpec((1, d), lambda i, f: (0, 0)),
                  pl.BlockSpec(g_arr.shape, lambda i, f: (0, 0))],
        out_specs=[row] * (1 + n_next),
        out_shape=outs,
        scratch_shapes=[pltpu.VMEM((tm, d), F32)],
        compiler_params=_params("arbitrary", "arbitrary"),
        name="mlp_residual",
    )(a, w_up, w_down, h, g_post, g_arr)


def _proj_kernel(a_ref, w_ref, *out_refs, scale):
    y = jnp.dot(a_ref[...], w_ref[...], preferred_element_type=F32)
    if scale is not None:
        y = y * scale
    for ref in out_refs:
        ref[...] = y.astype(ref.dtype)


def _proj(a, w, *, col0, n_out, out_dtypes, tm, tn, scale=None):
    n, d = a.shape
    assert col0 % tn == 0 and n_out % tn == 0
    jb = col0 // tn
    tile = pl.BlockSpec((tm, tn), lambda i, j: (i, j))
    return pl.pallas_call(
        functools.partial(_proj_kernel, scale=scale),
        grid=(n // tm, n_out // tn),
        in_specs=[pl.BlockSpec((tm, d), lambda i, j: (i, 0)),
                  pl.BlockSpec((d, tn), lambda i, j: (0, j + jb))],
        out_specs=[tile] * len(out_dtypes),
        out_shape=[jax.ShapeDtypeStruct((n, n_out), dt) for dt in out_dtypes],
        compiler_params=_params("arbitrary", "arbitrary"),
        name="proj",
    )(a, w)


def _logf_kernel(a_ref, wf_ref, bf_ref, lf_ref, c_ref, carry_s, *, tm, tiles_per_seq, seq_len):
    i = pl.program_id(0)
    x = jnp.dot(a_ref[...], wf_ref[...], preferred_element_type=F32) + bf_ref[...]
    lf = jnp.minimum(x, 0.0) - jnp.log1p(jnp.exp(-jnp.abs(x)))
    lf_ref[...] = lf
    r = lax.broadcasted_iota(jnp.int32, (tm, tm), 0)
    c = lax.broadcasted_iota(jnp.int32, (tm, tm), 1)
    same_seq = (r // seq_len) == (c // seq_len) if seq_len < tm else (r >= 0)
    tri = jnp.where((c <= r) & same_seq, 1.0, 0.0).astype(BF16)
    csum = _dot01_exact(tri, lf)

    @pl.when(i % tiles_per_seq == 0)
    def _():
        carry_s[...] = jnp.zeros(carry_s.shape, F32)

    csum = csum + carry_s[0:1, :]
    c_ref[...] = csum
    carry_s[...] = jnp.broadcast_to(csum[tm - 1:tm, :], carry_s.shape)


def _logf(a, w_f, b_f, *, tm, seq_len):
    n, d = a.shape
    assert seq_len % tm == 0 or tm % seq_len == 0
    tile = pl.BlockSpec((tm, LANES), lambda i: (i, 0))
    kern = functools.partial(_logf_kernel, tm=tm, tiles_per_seq=max(seq_len // tm, 1), seq_len=seq_len)
    return pl.pallas_call(
        kern,
        grid=(n // tm,),
        in_specs=[pl.BlockSpec((tm, d), lambda i: (i, 0)),
                  pl.BlockSpec((d, LANES), lambda i: (0, 0)),
                  pl.BlockSpec((1, LANES), lambda i: (0, 0))],
        out_specs=[tile, tile],
        out_shape=[jax.ShapeDtypeStruct((n, LANES), F32)] * 2,
        scratch_shapes=[pltpu.VMEM((SUBLANES, LANES), F32)],
        compiler_params=_params("arbitrary"),
        name="logf",
    )(a, w_f, b_f)


def _online_softmax_update(chunks, cq, m_old, l_old):
    mx = functools.reduce(jnp.maximum, chunks)
    rowmax = jnp.broadcast_to(jnp.max(mx, axis=-1, keepdims=True), mx.shape)
    m_new = jnp.maximum(m_old, rowmax + cq)
    alpha = jnp.exp2(m_old - m_new)
    shift = m_new - cq
    p = [jnp.exp2(c - shift) for c in chunks]
    rowsum = jnp.broadcast_to(jnp.sum(functools.reduce(jnp.add, p), axis=-1, keepdims=True), mx.shape)
    return m_new, alpha * l_old + rowsum, alpha, p


def _fox_prompt_kernel(q_ref, k_ref, v_ref, cq_ref, ck_ref, o_ref, m_s, l_s, acc_s, cqb_s, *, tq, tk):
    qi = pl.program_id(1)
    ki = pl.program_id(2)
    n_chunks = tk // LANES

    @pl.when(ki == 0)
    def _():
        m_s[...] = jnp.full(m_s.shape, -jnp.inf, F32)
        l_s[...] = jnp.zeros(l_s.shape, F32)
        acc_s[...] = jnp.zeros(acc_s.shape, F32)
        cq = cq_ref[0]
        for h in range(N_HEADS):
            cqb_s[h] = jnp.broadcast_to(cq[:, h:h + 1], cqb_s.shape[1:])

    def step(on_diagonal):
        ck = ck_ref[0]
        if on_diagonal:
            row = lax.broadcasted_iota(jnp.int32, (tq, LANES), 0)
            lane = lax.broadcasted_iota(jnp.int32, (tq, LANES), 1)
        for h in range(N_HEADS):
            cols = slice(h * HEAD_DIM, (h + 1) * HEAD_DIM)
            s = lax.dot_general(q_ref[0, :, cols], k_ref[0, :, cols],
                                (((1,), (1,)), ((), ())), preferred_element_type=F32)
            chunks = []
            for c in range(n_chunks):
                lanes = slice(c * LANES, (c + 1) * LANES)
                sc = s[:, lanes] - ck[h:h + 1, lanes]
                if on_diagonal:
                    sc = jnp.where(lane + c * LANES <= row, sc, NEG_BIG)
                chunks.append(sc)
            m_new, l_new, alpha, p = _online_softmax_update(chunks, cqb_s[h], m_s[h], l_s[h])
            m_s[h] = m_new
            l_s[h] = l_new
            pv = jnp.dot(jnp.concatenate(p, axis=1).astype(BF16), v_ref[0, :, cols],
                         preferred_element_type=F32)
            acc_s[:, cols] = alpha * acc_s[:, cols] + pv

    pl.when(ki < qi)(functools.partial(step, False))
    pl.when(ki == qi)(functools.partial(step, True))

    @pl.when(ki == pl.num_programs(2) - 1)
    def _():
        for h in range(N_HEADS):
            cols = slice(h * HEAD_DIM, (h + 1) * HEAD_DIM)
            o_ref[0, :, cols] = (acc_s[:, cols] / l_s[h]).astype(o_ref.dtype)


def _fox_prompt(q, k, v, c_rows, c_heads, *, tq, tk):
    b, t, d = q.shape
    assert tq == tk
    kv_map = lambda bi, qi, ki: (bi, jnp.minimum(ki, qi), 0)
    return pl.pallas_call(
        functools.partial(_fox_prompt_kernel, tq=tq, tk=tk),
        grid=(b, t // tq, t // tk),
        in_specs=[pl.BlockSpec((1, tq, d), lambda bi, qi, ki: (bi, qi, 0)),
                  pl.BlockSpec((1, tk, d), kv_map),
                  pl.BlockSpec((1, tk, d), kv_map),
                  pl.BlockSpec((1, tq, LANES), lambda bi, qi, ki: (bi, qi, 0)),
                  pl.BlockSpec((1, N_HEADS, tk), lambda bi, qi, ki: (bi, 0, jnp.minimum(ki, qi)))],
        out_specs=pl.BlockSpec((1, tq, d), lambda bi, qi, ki: (bi, qi, 0)),
        out_shape=jax.ShapeDtypeStruct((b, t, d), BF16),
        scratch_shapes=[pltpu.VMEM((N_HEADS, tq, LANES), F32),
                        pltpu.VMEM((N_HEADS, tq, LANES), F32),
                        pltpu.VMEM((tq, d), F32),
                        pltpu.VMEM((N_HEADS, tq, LANES), F32)],
        compiler_params=_params("arbitrary", "arbitrary", "arbitrary"),
        name="fox_prompt",
    )(q, k, v, c_rows, c_heads)


def _page_suffix_kernel(lf_ref, out_ref, *, heads):
    x = lf_ref[...]
    width = x.shape[1]
    lane = lax.broadcasted_iota(jnp.int32, x.shape, 1)
    incl = x
    k = heads
    while k < width:
        incl = incl + jnp.where(lane < width - k, pltpu.roll(incl, width - k, axis=1), 0.0)
        k *= 2
    total = jnp.where(lane < heads, incl, 0.0)
    k = heads
    while k < width:
        total = total + pltpu.roll(total, k, axis=1)
        k *= 2
    out_ref[:, 0, :] = (incl - x) * LOG2E
    out_ref[:, 1, :] = total * LOG2E


def _page_suffix(logf_flat, *, heads, pages_per_step):
    n_pool, width = logf_flat.shape
    assert n_pool % pages_per_step == 0
    return pl.pallas_call(
        functools.partial(_page_suffix_kernel, heads=heads),
        grid=(n_pool // pages_per_step,),
        in_specs=[pl.BlockSpec((pages_per_step, width), lambda i: (i, 0))],
        out_specs=pl.BlockSpec((pages_per_step, 2, width), lambda i: (i, 0, 0)),
        out_shape=jax.ShapeDtypeStruct((n_pool, 2, width), F32),
        compiler_params=_params("arbitrary"),
        name="page_suffix",
    )(logf_flat)


def _fox_sample_step(step, n_steps, q_ref, kn_ref, vn_ref, cq_ref, cnrow_ref, k_refs, v_refs, s_refs,
                     o_ref, m_s, l_s, acc_s, carry_s, *, joint, t_new):
    group = len(k_refs)
    rows = q_ref.shape[1]
    heads = rows // t_new
    width = carry_s.shape[1]
    nt = (((1,), (1,)), ((), ()))
    q = q_ref[0]
    cq = cq_ref[0]

    def own_head(n_lanes):
        lane_head = lax.broadcasted_iota(jnp.int32, (t_new, n_lanes), 1) % heads
        return [lane_head == h for h in range(heads)]

    def compact(s, masks):
        out = s[(heads - 1) * t_new:, :]
        for h in range(heads - 2, -1, -1):
            out = jnp.where(masks[h], s[h * t_new:(h + 1) * t_new, :], out)
        return out

    def expand(p, masks):
        return jnp.concatenate([jnp.where(masks[h], p, 0.0) for h in range(heads)], axis=0)

    def per_head_rows(x):
        return jnp.concatenate([jnp.broadcast_to(x[:, h:h + 1], (t_new, LANES)) for h in range(heads)],
                               axis=0)

    def over_positions(x, op):
        k = heads
        while k < LANES:
            x = op(x, pltpu.roll(x, k, axis=1))
            k *= 2
        return x

    def update(blocks):
        tiles = [c[:, i * LANES:(i + 1) * LANES] for c, _, _ in blocks for i in range(c.shape[1] // LANES)]
        m_old = m_s[...]
        m_new = jnp.maximum(m_old, over_positions(functools.reduce(jnp.maximum, tiles), jnp.maximum) + cq)
        alpha = jnp.exp2(m_old - m_new)
        shift = m_new - cq
        acc = per_head_rows(alpha) * acc_s[...]
        l_add = jnp.zeros_like(m_old)
        start = 0
        for c, v, masks in blocks:
            n = c.shape[1] // LANES
            p_tiles = [jnp.exp2(t - shift) for t in tiles[start:start + n]]
            start += n
            l_add = l_add + functools.reduce(jnp.add, p_tiles)
            acc = acc + jnp.dot(expand(jnp.concatenate(p_tiles, axis=1), masks), v,
                                preferred_element_type=F32)
        m_s[...] = m_new
        l_s[...] = alpha * l_s[...] + over_positions(l_add, jnp.add)
        acc_s[...] = acc

    @pl.when(step == 0)
    def _():
        m_s[...] = jnp.full(m_s.shape, -jnp.inf, F32)
        l_s[...] = jnp.zeros(l_s.shape, F32)
        acc_s[...] = jnp.zeros(acc_s.shape, F32)
        carry_s[...] = jnp.zeros(carry_s.shape, F32)
        masks = own_head(LANES)
        s = compact(lax.dot_general(q, kn_ref[0], nt, preferred_element_type=F32), masks) - cnrow_ref[0]
        t_q = lax.broadcasted_iota(jnp.int32, s.shape, 0)
        t_k = lax.broadcasted_iota(jnp.int32, s.shape, 1) // heads
        update([(jnp.where(t_k <= t_q, s, NEG_BIG), vn_ref[0], masks)])

    masks = own_head(width)
    carry = carry_s[...]
    for g0 in range(0, group, joint):
        blocks = []
        for g in range(g0, g0 + joint):
            sfx = s_refs[g][0]
            kf = k_refs[g][0].reshape(width, HEAD_DIM)
            vf = v_refs[g][0].reshape(width, HEAD_DIM)
            s = compact(lax.dot_general(q, kf, nt, preferred_element_type=F32), masks)
            blocks.append((s + (carry + sfx[0:1, :]), vf, masks))
            carry = carry + sfx[1:2, :]
        update(blocks)
    carry_s[...] = carry

    @pl.when(step == n_steps - 1)
    def _():
        o_ref[0] = (acc_s[...] / per_head_rows(l_s[...])).astype(o_ref.dtype)


def _fox_sample_kernel(pt_ref, q_ref, kn_ref, vn_ref, cq_ref, cnrow_ref, *refs, group, joint, t_new):
    k_refs = refs[:group]
    v_refs = refs[group:2 * group]
    s_refs = refs[2 * group:3 * group]
    o_ref = refs[3 * group]
    _fox_sample_step(pl.program_id(1), pl.num_programs(1), q_ref, kn_ref, vn_ref, cq_ref, cnrow_ref,
                     k_refs, v_refs, s_refs, o_ref, *refs[3 * group + 1:], joint=joint, t_new=t_new)


def _fox_sample(page_table, q, k_new, v_new, cq_tile, cn_row, cache_k, cache_v, page_sfx,
                *, group, joint, t_new):
    b, rows, dh = q.shape
    n_pages = page_table.shape[1]
    _, page, heads, _ = cache_k.shape
    width = page * heads
    assert n_pages % group == 0 and group % joint == 0 and rows == heads * t_new and rows == LANES
    assert LANES % heads == 0

    def page_map(g, ndim):
        return lambda bi, si, pt: (pt[bi, n_pages - 1 - (si * group + g)],) + (0,) * (ndim - 1)

    per_seq = lambda shape: pl.BlockSpec((1,) + shape, lambda bi, si, pt: (bi, 0, 0))
    kv_specs = [pl.BlockSpec((1, page, heads, dh), page_map(g, 4)) for g in range(group)]
    sfx_specs = [pl.BlockSpec((1, 2, width), page_map(g, 3)) for g in range(group)]
    stat = pltpu.VMEM((t_new, LANES), F32)
    grid_spec = pltpu.PrefetchScalarGridSpec(
        num_scalar_prefetch=1,
        grid=(b, n_pages // group),
        in_specs=[per_seq((rows, dh)), per_seq((rows, dh)), per_seq((rows, dh)),
                  per_seq((t_new, LANES)), per_seq((1, rows)),
                  *kv_specs, *kv_specs, *sfx_specs],
        out_specs=per_seq((rows, dh)),
        scratch_shapes=[stat, stat, pltpu.VMEM((rows, dh), F32), pltpu.VMEM((1, width), F32)],
    )
    return pl.pallas_call(
        functools.partial(_fox_sample_kernel, group=group, joint=joint, t_new=t_new),
        grid_spec=grid_spec,
        out_shape=jax.ShapeDtypeStruct((b, rows, dh), BF16),
        compiler_params=_params("arbitrary", "arbitrary"),
        name="fox_sample",
    )(page_table, q, k_new, v_new, cq_tile, cn_row,
      *([cache_k] * group), *([cache_v] * group), *([page_sfx] * group))


def _layer0_and_kv(x, seq_len, hist, kv16, q_dtype, w, *, tm):
    n, d = x.shape
    gated, u_rows = _inproj_conv(x, w["g_mix_pre"][0:1], w["w_in"], w["w_conv"],
                                 tm=tm, tn=512, seq_len=seq_len, hist=hist)
    row_tm = min(n, 512)
    h, a = _outproj_residual(gated, w["w_out"], x, w["g_mix_post"][0:1], w["g_ffn_pre"][0:1], tm=row_tm)
    g_next = jnp.concatenate([w["g_kv"], w["g_mix_pre"][1:2]], axis=0)
    h, a_kv, a_q = _mlp_residual(a, w["w_up"][0], w["w_down"][0], h, w["g_ffn_post"][0:1], g_next,
                                 tm=row_tm, tf=512)
    ptm = min(n, 1024)
    kv_dtypes = (F32, BF16) if kv16 else (F32,)
    k = _proj(a_kv, w["w_kvf"], col0=0, n_out=d, out_dtypes=kv_dtypes, tm=ptm, tn=1024)
    v = _proj(a_kv, w["w_kvf"], col0=d, n_out=d, out_dtypes=kv_dtypes, tm=ptm, tn=1024)
    lf, c = _logf(a_kv, w["w_f"], w["b_f"], tm=min(n, 256), seq_len=seq_len)
    (q,) = _proj(a_q, w["w_q"], col0=0, n_out=d, out_dtypes=(q_dtype,), tm=ptm, tn=1024,
                 scale=LOG2E * HEAD_DIM ** -0.5)
    return h, u_rows, k, v, lf, q, c * LOG2E


def kernel(x_prompt, x_sample, state_conv, cache_k, cache_v, cache_logf, page_table,
           w_in_a, w_conv_a, w_out_a, g_kv, w_kvf, b_f, w_q, w_o,
           g_mix_pre, g_mix_post, g_ffn_pre, g_ffn_post, w_up, w_down):
    bp, t, d = x_prompt.shape
    bs, ts, _ = x_sample.shape
    heads = N_HEADS
    w = {
        "w_in": w_in_a[0].astype(BF16), "w_conv": w_conv_a[0], "w_out": w_out_a[0].astype(BF16),
        "g_kv": g_kv[None, :], "w_kvf": w_kvf.astype(BF16),
        "w_f": jnp.pad(w_kvf[:, 2 * d:], ((0, 0), (0, LANES - heads))).astype(BF16),
        "b_f": jnp.pad(b_f, (0, LANES - heads))[None, :],
        "w_q": w_q[0].astype(BF16), "w_o": w_o[0].astype(BF16),
        "g_mix_pre": g_mix_pre, "g_mix_post": g_mix_post,
        "g_ffn_pre": g_ffn_pre, "g_ffn_post": g_ffn_post,
        "w_up": [w_up[layer].astype(BF16) for layer in range(w_up.shape[0])],
        "w_down": [w_down[layer].astype(BF16) for layer in range(w_down.shape[0])],
    }

    n_s = bs * ts
    prev = state_conv[0]
    zeros = jnp.zeros((bs, ts - 2, d), F32)
    p1 = jnp.concatenate([prev[:, 1:2], zeros, zeros[:, :1]], axis=1).reshape(n_s, d)
    p2 = jnp.concatenate([prev, zeros], axis=1).reshape(n_s, d)
    h_s, u_s, (k_s,), (v_s,), lf_s, q_s, c_s = _layer0_and_kv(
        x_sample.reshape(n_s, d), ts, (p1, p2), False, F32, w, tm=n_s)
    conv_sample = u_s.reshape(bs, ts, d)[:, ts - (CONV_W - 1):, :][None]
    n_pool, page = cache_logf.shape[:2]
    page_sfx = _page_suffix(cache_logf.reshape(n_pool, page * heads), heads=heads,
                            pages_per_step=64 if n_pool % 64 == 0 else n_pool)
    q_rows = q_s.reshape(bs, ts, heads, HEAD_DIM).transpose(0, 2, 1, 3).reshape(bs, heads * ts, HEAD_DIM)
    cn = c_s[:, :heads].reshape(bs, ts, heads)

    n_p = bp * t
    tm_p = min(t, 1024)
    h_p, u_p, (k_p, k16), (v_p, v16), lf_p, q_p, c_p = _layer0_and_kv(
        x_prompt.reshape(n_p, d), t, None, True, BF16, w, tm=tm_p)
    conv_prompt = u_p.reshape(bp, t // tm_p, SUBLANES, d)[:, -1, SUBLANES - (CONV_W - 1):, :][None]
    c3 = c_p.reshape(bp, t, LANES)
    o_p = _fox_prompt(q_p.reshape(bp, t, d), k16.reshape(bp, t, d), v16.reshape(bp, t, d),
                      c3, jnp.swapaxes(c3[:, :, :heads], 1, 2), tq=512, tk=512)
    h_p, a_p = _outproj_residual(o_p.reshape(n_p, d), w["w_o"], h_p, w["g_mix_post"][1:2],
                                 w["g_ffn_pre"][1:2], tm=512)

    (y_p,) = _mlp_residual(a_p, w["w_up"][1], w["w_down"][1], h_p, w["g_ffn_post"][1:2], None,
                           tm=512, tf=512)

    o_s = _fox_sample(page_table, q_rows,
                      k_s.reshape(bs, ts * heads, HEAD_DIM), v_s.reshape(bs, ts * heads, HEAD_DIM),
                      jnp.tile(cn, (1, 1, LANES // heads)), cn.reshape(bs, 1, ts * heads),
                      cache_k, cache_v, page_sfx, group=8, joint=SAMPLE_JOINT, t_new=ts)
    o_s = o_s.reshape(bs, heads, ts, HEAD_DIM).transpose(0, 2, 1, 3).reshape(n_s, d)
    h_s, a_s = _outproj_residual(o_s, w["w_o"], h_s, w["g_mix_post"][1:2], w["g_ffn_pre"][1:2], tm=n_s)
    (y_s,) = _mlp_residual(a_s, w["w_up"][1], w["w_down"][1], h_s, w["g_ffn_post"][1:2], None,
                           tm=n_s, tf=512)

    return (y_p.reshape(bp, t, d), y_s.reshape(bs, ts, d), conv_prompt, conv_sample,
            k_p.reshape(bp, t, heads, HEAD_DIM), v_p.reshape(bp, t, heads, HEAD_DIM),
            lf_p[:, :heads].reshape(bp, t, heads),
            k_s.reshape(bs, ts, heads, HEAD_DIM), v_s.reshape(bs, ts, heads, HEAD_DIM),
            lf_s[:, :heads].reshape(bs, ts, heads))
```

```python
import functools
import math

import jax
import jax.numpy as jnp
from jax import lax
from jax.experimental import pallas as pl
from jax.experimental.pallas import tpu as pltpu

RMS_EPS = 1e-6
CONV_W = 3
N_HEADS = 16
HEAD_DIM = 128
LANES = 128
SUBLANES = 8
VMEM_LIMIT = 60 * 1024 * 1024
NEG_BIG = -1e30
LOG2E = math.log2(math.e)
SAMPLE_JOINT = 4

F32 = jnp.float32
BF16 = jnp.bfloat16


def _params(*sem):
    return pltpu.CompilerParams(dimension_semantics=sem, vmem_limit_bytes=VMEM_LIMIT)


def _rms(x, g):
    return x * lax.rsqrt(jnp.mean(x * x, axis=-1, keepdims=True) + RMS_EPS) * g


def _split3(x):
    a = x.astype(BF16)
    r = x - a.astype(F32)
    b = r.astype(BF16)
    c = (r - b.astype(F32)).astype(BF16)
    return a, b, c


def _dot01_exact(m01, x):
    a, b, c = _split3(x)
    d = functools.partial(jnp.dot, preferred_element_type=F32)
    return d(m01, a) + d(m01, b) + d(m01, c)


def _inproj_conv_kernel(*refs, tm, tiles_per_seq, seq_len, inject):
    if inject:
        (x_ref, g_ref, wb_ref, wc_ref, wx_ref, wconv_ref, p1_ref, p2_ref,
         gated_ref, u_ref, a_s) = refs
    else:
        (x_ref, g_ref, wb_ref, wc_ref, wx_ref, wconv_ref,
         gated_ref, u_ref, a_s, carry_s) = refs
    i = pl.program_id(0)
    j = pl.program_id(1)

    @pl.when(j == 0)
    def _():
        a_s[...] = _rms(x_ref[...], g_ref[...]).astype(BF16)

    a = a_s[...]
    dot = functools.partial(jnp.dot, preferred_element_type=F32)
    u = dot(a, wc_ref[...]) * dot(a, wx_ref[...])
    rolled1 = pltpu.roll(u, 1, axis=0)
    rolled2 = pltpu.roll(u, 2, axis=0)
    row = lax.broadcasted_iota(jnp.int32, u.shape, 0)
    if inject:
        pos = row % seq_len
        u1 = jnp.where(pos >= 1, rolled1, p1_ref[...])
        u2 = jnp.where(pos >= 2, rolled2, p2_ref[...])
        u_ref[...] = u
    else:
        @pl.when(i % tiles_per_seq == 0)
        def _():
            carry_s[j] = jnp.zeros(carry_s.shape[1:], F32)

        tail = carry_s[j]
        t1 = tail[SUBLANES - 1:SUBLANES, :]
        t2 = tail[SUBLANES - 2:SUBLANES - 1, :]
        u1 = jnp.where(row == 0, t1, rolled1)
        u2 = jnp.where(row == 0, t2, jnp.where(row == 1, t1, rolled2))
        last = u[tm - SUBLANES:, :]
        carry_s[j] = last
        u_ref[...] = last
    wconv = wconv_ref[...]
    conv = wconv[0:1, :] * u2 + wconv[1:2, :] * u1 + wconv[2:3, :] * u
    gated_ref[...] = (dot(a, wb_ref[...]) * conv).astype(BF16)


def _inproj_conv(x, g, w_in, w_conv, *, tm, tn, seq_len, hist=None):
    n, d = x.shape
    nj = d // tn
    inject = hist is not None
    assert n % tm == 0 and d % tn == 0
    if not inject:
        assert seq_len % tm == 0
    x_spec = pl.BlockSpec((tm, d), lambda i, j: (i, 0))
    g_spec = pl.BlockSpec((1, d), lambda i, j: (0, 0))
    w_specs = [pl.BlockSpec((d, tn), lambda i, j, s=s: (0, j + s * nj)) for s in range(3)]
    wconv_spec = pl.BlockSpec((CONV_W, tn), lambda i, j: (0, j))
    tile_spec = pl.BlockSpec((tm, tn), lambda i, j: (i, j))
    in_specs = [x_spec, g_spec, *w_specs, wconv_spec]
    args = [x, g, w_in, w_in, w_in, w_conv]
    scratch = [pltpu.VMEM((tm, d), BF16)]
    if inject:
        in_specs += [tile_spec, tile_spec]
        args += list(hist)
        u_shape = jax.ShapeDtypeStruct((n, d), F32)
        u_spec = tile_spec
    else:
        scratch.append(pltpu.VMEM((nj, SUBLANES, tn), F32))
        u_shape = jax.ShapeDtypeStruct((n // tm * SUBLANES, d), F32)
        u_spec = pl.BlockSpec((SUBLANES, tn), lambda i, j: (i, j))
    kern = functools.partial(_inproj_conv_kernel, tm=tm, tiles_per_seq=max(seq_len // tm, 1),
                             seq_len=seq_len, inject=inject)
    return pl.pallas_call(
        kern,
        grid=(n // tm, nj),
        in_specs=in_specs,
        out_specs=[tile_spec, u_spec],
        out_shape=[jax.ShapeDtypeStruct((n, d), BF16), u_shape],
        scratch_shapes=scratch,
        compiler_params=_params("arbitrary", "arbitrary"),
        name="inproj_conv",
    )(*args)


def _outproj_kernel(a_ref, w_ref, h_ref, gpost_ref, gnext_ref, h1_ref, anext_ref):
    mix = jnp.dot(a_ref[...], w_ref[...], preferred_element_type=F32)
    h1 = h_ref[...] + _rms(mix, gpost_ref[...])
    h1_ref[...] = h1
    anext_ref[...] = _rms(h1, gnext_ref[...]).astype(BF16)


def _outproj_residual(a, w, h, g_post, g_next, *, tm):
    n, d = h.shape
    row = pl.BlockSpec((tm, d), lambda i: (i, 0))
    vec = pl.BlockSpec((1, d), lambda i: (0, 0))
    return pl.pallas_call(
        _outproj_kernel,
        grid=(n // tm,),
        in_specs=[row, pl.BlockSpec((d, d), lambda i: (0, 0)), row, vec, vec],
        out_specs=[row, row],
        out_shape=[jax.ShapeDtypeStruct((n, d), F32), jax.ShapeDtypeStruct((n, d), BF16)],
        compiler_params=_params("arbitrary"),
        name="outproj_residual",
    )(a, w, h, g_post, g_next)


def _mlp_kernel(*refs, n_next, own_acc):
    a_ref, wup_ref, wdown_ref, h_ref, gpost_ref, gnext_ref, h2_ref = refs[:7]
    next_refs = refs[7:7 + n_next]
    acc = refs[7 + n_next] if own_acc else h2_ref
    f = pl.program_id(1)

    @pl.when(f == 0)
    def _():
        acc[...] = jnp.zeros(acc.shape, F32)

    hid = jnp.dot(a_ref[...], wup_ref[...], preferred_element_type=F32)
    hid = jnp.square(jnp.maximum(hid, 0.0)).astype(BF16)
    acc[...] += jnp.dot(hid, wdown_ref[...], preferred_element_type=F32)

    @pl.when(f == pl.num_programs(1) - 1)
    def _():
        h2 = h_ref[...] + _rms(acc[...], gpost_ref[...])
        h2_ref[...] = h2
        for k, ref in enumerate(next_refs):
            ref[...] = _rms(h2, gnext_ref[k:k + 1, :]).astype(BF16)


def _mlp_residual(a, w_up, w_down, layer, h, g_post, g_next, *, tm, tf):
    n, d = h.shape
    d_ff = w_up.shape[2]
    n_next = 0 if g_next is None else g_next.shape[0]
    g_arr = jnp.ones((1, d), F32) if g_next is None else g_next
    own_acc = n_next > 0
    row = pl.BlockSpec((tm, d), lambda i, f: (i, 0))
    outs = [jax.ShapeDtypeStruct((n, d), F32)] + [jax.ShapeDtypeStruct((n, d), BF16)] * n_next
    return pl.pallas_call(
        functools.partial(_mlp_kernel, n_next=n_next, own_acc=own_acc),
        grid=(n // tm, d_ff // tf),
        in_specs=[row,
                  pl.BlockSpec((None, d, tf), lambda i, f: (layer, 0, f)),
                  pl.BlockSpec((None, tf, d), lambda i, f: (layer, f, 0)),
                  row,
                  pl.BlockSpec((1, d), lambda i, f: (0, 0)),
                  pl.BlockSpec(g_arr.shape, lambda i, f: (0, 0))],
        out_specs=[row] * (1 + n_next),
        out_shape=outs,
        scratch_shapes=[pltpu.VMEM((tm, d), F32)] if own_acc else [],
        compiler_params=_params("arbitrary", "arbitrary"),
        name="mlp_residual",
    )(a, w_up, w_down, h, g_post, g_arr)


def _proj_kernel(a_ref, w_ref, *out_refs, scale):
    y = jnp.dot(a_ref[...], w_ref[...], preferred_element_type=F32)
    if scale is not None:
        y = y * scale
    for ref in out_refs:
        if len(ref.shape) == 3:
            per_head = [y[:, hh * HEAD_DIM:(hh + 1) * HEAD_DIM] for hh in range(ref.shape[1])]
            ref[...] = jnp.swapaxes(jnp.stack(per_head, axis=0), 0, 1).astype(ref.dtype)
        else:
            ref[...] = y.astype(ref.dtype)


def _proj(a, w, *, col0, n_out, out_dtypes, tm, tn, scale=None, split_heads=()):
    n, d = a.shape
    assert col0 % tn == 0 and n_out % tn == 0 and tn % HEAD_DIM == 0
    jb = col0 // tn
    tile = pl.BlockSpec((tm, tn), lambda i, j: (i, j))
    head_tile = pl.BlockSpec((tm, tn // HEAD_DIM, HEAD_DIM), lambda i, j: (i, j, 0))
    out_specs, out_shape = [], []
    for k, dt in enumerate(out_dtypes):
        if k in split_heads:
            out_specs.append(head_tile)
            out_shape.append(jax.ShapeDtypeStruct((n, n_out // HEAD_DIM, HEAD_DIM), dt))
        else:
            out_specs.append(tile)
            out_shape.append(jax.ShapeDtypeStruct((n, n_out), dt))
    return pl.pallas_call(
        functools.partial(_proj_kernel, scale=scale),
        grid=(n // tm, n_out // tn),
        in_specs=[pl.BlockSpec((tm, d), lambda i, j: (i, 0)),
                  pl.BlockSpec((d, tn), lambda i, j: (0, j + jb))],
        out_specs=out_specs,
        out_shape=out_shape,
        compiler_params=_params("arbitrary", "arbitrary"),
        name="proj",
    )(a, w)


def _logf_kernel(a_ref, wf_ref, bf_ref, lf_ref, c_ref, carry_s, *, tm, tiles_per_seq, seq_len):
    i = pl.program_id(0)
    x = jnp.dot(a_ref[...], wf_ref[...], preferred_element_type=F32) + bf_ref[...]
    lf = jnp.minimum(x, 0.0) - jnp.log1p(jnp.exp(-jnp.abs(x)))
    lf_ref[...] = lf
    r = lax.broadcasted_iota(jnp.int32, (tm, tm), 0)
    c = lax.broadcasted_iota(jnp.int32, (tm, tm), 1)
    same_seq = (r // seq_len) == (c // seq_len) if seq_len < tm else (r >= 0)
    tri = jnp.where((c <= r) & same_seq, 1.0, 0.0).astype(BF16)
    csum = _dot01_exact(tri, lf)

    @pl.when(i % tiles_per_seq == 0)
    def _():
        carry_s[...] = jnp.zeros(carry_s.shape, F32)

    csum = csum + carry_s[0:1, :]
    c_ref[...] = csum
    carry_s[...] = jnp.broadcast_to(csum[tm - 1:tm, :], carry_s.shape)


def _logf(a, w_f, b_f, *, tm, seq_len):
    n, d = a.shape
    assert seq_len % tm == 0 or tm % seq_len == 0
    tile = pl.BlockSpec((tm, LANES), lambda i: (i, 0))
    kern = functools.partial(_logf_kernel, tm=tm, tiles_per_seq=max(seq_len // tm, 1), seq_len=seq_len)
    return pl.pallas_call(
        kern,
        grid=(n // tm,),
        in_specs=[pl.BlockSpec((tm, d), lambda i: (i, 0)),
                  pl.BlockSpec((d, LANES), lambda i: (0, 0)),
                  pl.BlockSpec((1, LANES), lambda i: (0, 0))],
        out_specs=[tile, tile],
        out_shape=[jax.ShapeDtypeStruct((n, LANES), F32)] * 2,
        scratch_shapes=[pltpu.VMEM((SUBLANES, LANES), F32)],
        compiler_params=_params("arbitrary"),
        name="logf",
    )(a, w_f, b_f)


def _online_softmax_update(chunks, cq, m_old, l_old):
    mx = functools.reduce(jnp.maximum, chunks)
    rowmax = jnp.broadcast_to(jnp.max(mx, axis=-1, keepdims=True), mx.shape)
    m_new = jnp.maximum(m_old, rowmax + cq)
    alpha = jnp.exp2(m_old - m_new)
    shift = m_new - cq
    p = [jnp.exp2(c - shift) for c in chunks]
    rowsum = jnp.broadcast_to(jnp.sum(functools.reduce(jnp.add, p), axis=-1, keepdims=True), mx.shape)
    return m_new, alpha * l_old + rowsum, alpha, p


def _fox_prompt_kernel(q_ref, k_ref, v_ref, cq_ref, ck_ref, o_ref, m_s, l_s, acc_s, cqb_s, *, tq, tk):
    qi = pl.program_id(1)
    ki = pl.program_id(2)
    n_chunks = tk // LANES

    @pl.when(ki == 0)
    def _():
        m_s[...] = jnp.full(m_s.shape, -jnp.inf, F32)
        l_s[...] = jnp.zeros(l_s.shape, F32)
        acc_s[...] = jnp.zeros(acc_s.shape, F32)
        cq = cq_ref[0]
        for h in range(N_HEADS):
            cqb_s[h] = jnp.broadcast_to(cq[:, h:h + 1], cqb_s.shape[1:])

    def step(on_diagonal):
        ck = ck_ref[0]
        if on_diagonal:
            row = lax.broadcasted_iota(jnp.int32, (tq, LANES), 0)
            lane = lax.broadcasted_iota(jnp.int32, (tq, LANES), 1)
        for h in range(N_HEADS):
            cols = slice(h * HEAD_DIM, (h + 1) * HEAD_DIM)
            s = lax.dot_general(q_ref[0, :, cols], k_ref[0, :, cols],
                                (((1,), (1,)), ((), ())), preferred_element_type=F32)
            chunks = []
            for c in range(n_chunks):
                lanes = slice(c * LANES, (c + 1) * LANES)
                sc = s[:, lanes] - ck[h:h + 1, lanes]
                if on_diagonal:
                    sc = jnp.where(lane + c * LANES <= row, sc, NEG_BIG)
                chunks.append(sc)
            m_new, l_new, alpha, p = _online_softmax_update(chunks, cqb_s[h], m_s[h], l_s[h])
            m_s[h] = m_new
            l_s[h] = l_new
            pv = jnp.dot(jnp.concatenate(p, axis=1).astype(BF16), v_ref[0, :, cols],
                         preferred_element_type=F32)
            acc_s[:, cols] = alpha * acc_s[:, cols] + pv

    pl.when(ki < qi)(functools.partial(step, False))
    pl.when(ki == qi)(functools.partial(step, True))

    @pl.when(ki == pl.num_programs(2) - 1)
    def _():
        for h in range(N_HEADS):
            cols = slice(h * HEAD_DIM, (h + 1) * HEAD_DIM)
            o_ref[0, :, cols] = (acc_s[:, cols] / l_s[h]).astype(o_ref.dtype)


def _fox_prompt(q, k, v, c_rows, c_heads, *, tq, tk):
    b, t, d = q.shape
    assert tq == tk
    kv_map = lambda bi, qi, ki: (bi, jnp.minimum(ki, qi), 0)
    return pl.pallas_call(
        functools.partial(_fox_prompt_kernel, tq=tq, tk=tk),
        grid=(b, t // tq, t // tk),
        in_specs=[pl.BlockSpec((1, tq, d), lambda bi, qi, ki: (bi, qi, 0)),
                  pl.BlockSpec((1, tk, d), kv_map),
                  pl.BlockSpec((1, tk, d), kv_map),
                  pl.BlockSpec((1, tq, LANES), lambda bi, qi, ki: (bi, qi, 0)),
                  pl.BlockSpec((1, N_HEADS, tk), lambda bi, qi, ki: (bi, 0, jnp.minimum(ki, qi)))],
        out_specs=pl.BlockSpec((1, tq, d), lambda bi, qi, ki: (bi, qi, 0)),
        out_shape=jax.ShapeDtypeStruct((b, t, d), BF16),
        scratch_shapes=[pltpu.VMEM((N_HEADS, tq, LANES), F32),
                        pltpu.VMEM((N_HEADS, tq, LANES), F32),
                        pltpu.VMEM((tq, d), F32),
                        pltpu.VMEM((N_HEADS, tq, LANES), F32)],
        compiler_params=_params("arbitrary", "arbitrary", "arbitrary"),
        name="fox_prompt",
    )(q, k, v, c_rows, c_heads)


def _page_suffix_kernel(lf_ref, out_ref, *, heads):
    x = lf_ref[...]
    width = x.shape[1]
    lane = lax.broadcasted_iota(jnp.int32, x.shape, 1)
    incl = x
    k = heads
    while k < width:
        incl = incl + jnp.where(lane < width - k, pltpu.roll(incl, width - k, axis=1), 0.0)
        k *= 2
    total = jnp.where(lane < heads, incl, 0.0)
    k = heads
    while k < width:
        total = total + pltpu.roll(total, k, axis=1)
        k *= 2
    out_ref[:, 0, :] = (incl - x) * LOG2E
    out_ref[:, 1, :] = total * LOG2E


def _page_suffix(logf_flat, *, heads, pages_per_step):
    n_pool, width = logf_flat.shape
    assert n_pool % pages_per_step == 0
    return pl.pallas_call(
        functools.partial(_page_suffix_kernel, heads=heads),
        grid=(n_pool // pages_per_step,),
        in_specs=[pl.BlockSpec((pages_per_step, width), lambda i: (i, 0))],
        out_specs=pl.BlockSpec((pages_per_step, 2, width), lambda i: (i, 0, 0)),
        out_shape=jax.ShapeDtypeStruct((n_pool, 2, width), F32),
        compiler_params=_params("arbitrary"),
        name="page_suffix",
    )(logf_flat)


def _fox_sample_step(step, n_steps, q_ref, kn_ref, vn_ref, cq_ref, cnrow_ref, k_refs, v_refs, s_refs,
                     o_ref, m_s, l_s, acc_s, carry_s, *, joint, t_new):
    group = len(k_refs)
    rows = q_ref.shape[1]
    heads = rows // t_new
    width = carry_s.shape[1]
    nt = (((1,), (1,)), ((), ()))
    q = q_ref[0]
    cq = cq_ref[0]

    def own_head(n_lanes):
        lane_head = lax.broadcasted_iota(jnp.int32, (t_new, n_lanes), 1) % heads
        return [lane_head == h for h in range(heads)]

    def compact(s, masks):
        out = s[(heads - 1) * t_new:, :]
        for h in range(heads - 2, -1, -1):
            out = jnp.where(masks[h], s[h * t_new:(h + 1) * t_new, :], out)
        return out

    def expand(p, masks):
        return jnp.concatenate([jnp.where(masks[h], p, 0.0) for h in range(heads)], axis=0)

    def per_head_rows(x):
        return jnp.concatenate([jnp.broadcast_to(x[:, h:h + 1], (t_new, LANES)) for h in range(heads)],
                               axis=0)

    def over_positions(x, op):
        k = heads
        while k < LANES:
            x = op(x, pltpu.roll(x, k, axis=1))
            k *= 2
        return x

    def update(blocks):
        tiles = [c[:, i * LANES:(i + 1) * LANES] for c, _, _ in blocks for i in range(c.shape[1] // LANES)]
        m_old = m_s[...]
        m_new = jnp.maximum(m_old, over_positions(functools.reduce(jnp.maximum, tiles), jnp.maximum) + cq)
        alpha = jnp.exp2(m_old - m_new)
        shift = m_new - cq
        acc = per_head_rows(alpha) * acc_s[...]
        l_add = jnp.zeros_like(m_old)
        start = 0
        for c, v, masks in blocks:
            n = c.shape[1] // LANES
            p_tiles = [jnp.exp2(t - shift) for t in tiles[start:start + n]]
            start += n
            l_add = l_add + functools.reduce(jnp.add, p_tiles)
            acc = acc + jnp.dot(expand(jnp.concatenate(p_tiles, axis=1), masks), v,
                                preferred_element_type=F32)
        m_s[...] = m_new
        l_s[...] = alpha * l_s[...] + over_positions(l_add, jnp.add)
        acc_s[...] = acc

    @pl.when(step == 0)
    def _():
        m_s[...] = jnp.full(m_s.shape, -jnp.inf, F32)
        l_s[...] = jnp.zeros(l_s.shape, F32)
        acc_s[...] = jnp.zeros(acc_s.shape, F32)
        carry_s[...] = jnp.zeros(carry_s.shape, F32)
        masks = own_head(LANES)
        s = compact(lax.dot_general(q, kn_ref[0], nt, preferred_element_type=F32), masks) - cnrow_ref[0]
        t_q = lax.broadcasted_iota(jnp.int32, s.shape, 0)
        t_k = lax.broadcasted_iota(jnp.int32, s.shape, 1) // heads
        update([(jnp.where(t_k <= t_q, s, NEG_BIG), vn_ref[0], masks)])

    masks = own_head(width)
    carry = carry_s[...]
    for g0 in range(0, group, joint):
        blocks = []
        for g in range(g0, g0 + joint):
            sfx = s_refs[g][0]
            kf = k_refs[g][0].reshape(width, HEAD_DIM)
            vf = v_refs[g][0].reshape(width, HEAD_DIM)
            s = compact(lax.dot_general(q, kf, nt, preferred_element_type=F32), masks)
            blocks.append((s + (carry + sfx[0:1, :]), vf, masks))
            carry = carry + sfx[1:2, :]
        update(blocks)
    carry_s[...] = carry

    @pl.when(step == n_steps - 1)
    def _():
        o_ref[0] = (acc_s[...] / per_head_rows(l_s[...])).astype(o_ref.dtype)


def _fox_sample_kernel(pt_ref, q_ref, kn_ref, vn_ref, cq_ref, cnrow_ref, *refs, group, joint, t_new):
    k_refs = refs[:group]
    v_refs = refs[group:2 * group]
    s_refs = refs[2 * group:3 * group]
    o_ref = refs[3 * group]
    _fox_sample_step(pl.program_id(1), pl.num_programs(1), q_ref, kn_ref, vn_ref, cq_ref, cnrow_ref,
                     k_refs, v_refs, s_refs, o_ref, *refs[3 * group + 1:], joint=joint, t_new=t_new)


def _fox_sample(page_table, q, k_new, v_new, cq_tile, cn_row, cache_k, cache_v, page_sfx,
                *, group, joint, t_new):
    b, rows, dh = q.shape
    n_pages = page_table.shape[1]
    _, page, heads, _ = cache_k.shape
    width = page * heads
    assert n_pages % group == 0 and group % joint == 0 and rows == heads * t_new and rows == LANES
    assert LANES % heads == 0

    def page_map(g, ndim):
        return lambda bi, si, pt: (pt[bi, n_pages - 1 - (si * group + g)],) + (0,) * (ndim - 1)

    per_seq = lambda shape: pl.BlockSpec((1,) + shape, lambda bi, si, pt: (bi, 0, 0))
    kv_specs = [pl.BlockSpec((1, page, heads, dh), page_map(g, 4)) for g in range(group)]
    sfx_specs = [pl.BlockSpec((1, 2, width), page_map(g, 3)) for g in range(group)]
    stat = pltpu.VMEM((t_new, LANES), F32)
    grid_spec = pltpu.PrefetchScalarGridSpec(
        num_scalar_prefetch=1,
        grid=(b, n_pages // group),
        in_specs=[per_seq((rows, dh)), per_seq((rows, dh)), per_seq((rows, dh)),
                  per_seq((t_new, LANES)), per_seq((1, rows)),
                  *kv_specs, *kv_specs, *sfx_specs],
        out_specs=per_seq((rows, dh)),
        scratch_shapes=[stat, stat, pltpu.VMEM((rows, dh), F32), pltpu.VMEM((1, width), F32)],
    )
    return pl.pallas_call(
        functools.partial(_fox_sample_kernel, group=group, joint=joint, t_new=t_new),
        grid_spec=grid_spec,
        out_shape=jax.ShapeDtypeStruct((b, rows, dh), BF16),
        compiler_params=_params("arbitrary", "arbitrary"),
        name="fox_sample",
    )(page_table, q, k_new, v_new, cq_tile, cn_row,
      *([cache_k] * group), *([cache_v] * group), *([page_sfx] * group))


def _layer0_and_kv(x, seq_len, hist, kv16, q_dtype, w, *, tm):
    n, d = x.shape
    gated, u_rows = _inproj_conv(x, w["g_mix_pre"][0:1], w["w_in"], w["w_conv"],
                                 tm=tm, tn=512, seq_len=seq_len, hist=hist)
    row_tm = min(n, 512)
    h, a = _outproj_residual(gated, w["w_out"], x, w["g_mix_post"][0:1], w["g_ffn_pre"][0:1], tm=row_tm)
    g_next = jnp.concatenate([w["g_kv"], w["g_mix_pre"][1:2]], axis=0)
    h, a_kv, a_q = _mlp_residual(a, w["w_up"], w["w_down"], 0, h, w["g_ffn_post"][0:1], g_next,
                                 tm=row_tm, tf=512)
    ptm = min(n, 1024)
    kv_dtypes = (F32, BF16) if kv16 else (F32,)
    k = _proj(a_kv, w["w_kvf"], col0=0, n_out=d, out_dtypes=kv_dtypes, tm=ptm, tn=1024, split_heads=(0,))
    v = _proj(a_kv, w["w_kvf"], col0=d, n_out=d, out_dtypes=kv_dtypes, tm=ptm, tn=1024, split_heads=(0,))
    lf, c = _logf(a_kv, w["w_f"], w["b_f"], tm=min(n, 256), seq_len=seq_len)
    (q,) = _proj(a_q, w["w_q"], col0=0, n_out=d, out_dtypes=(q_dtype,), tm=ptm, tn=1024,
                 scale=LOG2E * HEAD_DIM ** -0.5)
    return h, u_rows, k, v, lf, q, c * LOG2E


def kernel(x_prompt, x_sample, state_conv, cache_k, cache_v, cache_logf, page_table,
           w_in_a, w_conv_a, w_out_a, g_kv, w_kvf, b_f, w_q, w_o,
           g_mix_pre, g_mix_post, g_ffn_pre, g_ffn_post, w_up, w_down):
    bp, t, d = x_prompt.shape
    bs, ts, _ = x_sample.shape
    heads = N_HEADS
    w = {
        "w_in": w_in_a[0].astype(BF16), "w_conv": w_conv_a[0], "w_out": w_out_a[0].astype(BF16),
        "g_kv": g_kv[None, :], "w_kvf": w_kvf.astype(BF16),
        "w_f": jnp.pad(w_kvf[:, 2 * d:], ((0, 0), (0, LANES - heads))).astype(BF16),
        "b_f": jnp.pad(b_f, (0, LANES - heads))[None, :],
        "w_q": w_q[0].astype(BF16), "w_o": w_o[0].astype(BF16),
        "g_mix_pre": g_mix_pre, "g_mix_post": g_mix_post,
        "g_ffn_pre": g_ffn_pre, "g_ffn_post": g_ffn_post,
        "w_up": w_up.astype(BF16), "w_down": w_down.astype(BF16),
    }

    n_s = bs * ts
    prev = state_conv[0]
    zeros = jnp.zeros((bs, ts - 2, d), F32)
    p1 = jnp.concatenate([prev[:, 1:2], zeros, zeros[:, :1]], axis=1).reshape(n_s, d)
    p2 = jnp.concatenate([prev, zeros], axis=1).reshape(n_s, d)
    h_s, u_s, (k_s,), (v_s,), lf_s, q_s, c_s = _layer0_and_kv(
        x_sample.reshape(n_s, d), ts, (p1, p2), False, F32, w, tm=n_s)
    conv_sample = u_s.reshape(bs, ts, d)[:, ts - (CONV_W - 1):, :][None]
    n_pool, page = cache_logf.shape[:2]
    page_sfx = _page_suffix(cache_logf.reshape(n_pool, page * heads), heads=heads,
                            pages_per_step=64 if n_pool % 64 == 0 else n_pool)
    q_rows = q_s.reshape(bs, ts, heads, HEAD_DIM).transpose(0, 2, 1, 3).reshape(bs, heads * ts, HEAD_DIM)
    cn = c_s[:, :heads].reshape(bs, ts, heads)

    n_p = bp * t
    tm_p = min(t, 1024)
    h_p, u_p, (k_p, k16), (v_p, v16), lf_p, q_p, c_p = _layer0_and_kv(
        x_prompt.reshape(n_p, d), t, None, True, BF16, w, tm=tm_p)
    conv_prompt = u_p.reshape(bp, t // tm_p, SUBLANES, d)[:, -1, SUBLANES - (CONV_W - 1):, :][None]
    c3 = c_p.reshape(bp, t, LANES)
    o_p = _fox_prompt(q_p.reshape(bp, t, d), k16.reshape(bp, t, d), v16.reshape(bp, t, d),
                      c3, jnp.swapaxes(c3[:, :, :heads], 1, 2), tq=512, tk=512)
    h_p, a_p = _outproj_residual(o_p.reshape(n_p, d), w["w_o"], h_p, w["g_mix_post"][1:2],
                                 w["g_ffn_pre"][1:2], tm=512)

    (y_p,) = _mlp_residual(a_p, w["w_up"], w["w_down"], 1, h_p, w["g_ffn_post"][1:2], None,
                           tm=min(n_p, 1024), tf=512)

    o_s = _fox_sample(page_table, q_rows,
                      k_s.reshape(bs, ts * heads, HEAD_DIM), v_s.reshape(bs, ts * heads, HEAD_DIM),
                      jnp.tile(cn, (1, 1, LANES // heads)), cn.reshape(bs, 1, ts * heads),
                      cache_k, cache_v, page_sfx, group=8, joint=SAMPLE_JOINT, t_new=ts)
    o_s = o_s.reshape(bs, heads, ts, HEAD_DIM).transpose(0, 2, 1, 3).reshape(n_s, d)
    h_s, a_s = _outproj_residual(o_s, w["w_o"], h_s, w["g_mix_post"][1:2], w["g_ffn_pre"][1:2], tm=n_s)
    (y_s,) = _mlp_residual(a_s, w["w_up"], w["w_down"], 1, h_s, w["g_ffn_post"][1:2], None,
                           tm=n_s, tf=512)

    return (y_p.reshape(bp, t, d), y_s.reshape(bs, ts, d), conv_prompt, conv_sample,
            k_p.reshape(bp, t, heads, HEAD_DIM), v_p.reshape(bp, t, heads, HEAD_DIM),
            lf_p[:, :heads].reshape(bp, t, heads),
            k_s.reshape(bs, ts, heads, HEAD_DIM), v_s.reshape(bs, ts, heads, HEAD_DIM),
            lf_s[:, :heads].reshape(bs, ts, heads))
```

```python
import functools
import math

import jax
import jax.numpy as jnp
from jax import lax
from jax.experimental import pallas as pl
from jax.experimental.pallas import tpu as pltpu

RMS_EPS = 1e-6
CONV_W = 3
N_HEADS = 16
HEAD_DIM = 128
LANES = 128
SUBLANES = 8
VMEM_LIMIT = 60 * 1024 * 1024
NEG_BIG = -1e30
LOG2E = math.log2(math.e)
SAMPLE_JOINT = 4

F32 = jnp.float32
BF16 = jnp.bfloat16


def _params(*sem):
    return pltpu.CompilerParams(dimension_semantics=sem, vmem_limit_bytes=VMEM_LIMIT)


def _rms(x, g):
    return x * lax.rsqrt(jnp.mean(x * x, axis=-1, keepdims=True) + RMS_EPS) * g


def _split3(x):
    a = x.astype(BF16)
    r = x - a.astype(F32)
    b = r.astype(BF16)
    c = (r - b.astype(F32)).astype(BF16)
    return a, b, c


def _dot01_exact(m01, x):
    a, b, c = _split3(x)
    d = functools.partial(jnp.dot, preferred_element_type=F32)
    return d(m01, a) + d(m01, b) + d(m01, c)


def _cast_specs(stacks, layer, n_steps, step_of):
    in_specs, out_specs, out_shape = [], [], []
    for stack in stacks:
        _, r, c = stack.shape
        rows = r // n_steps
        assert rows * n_steps == r and rows % (2 * SUBLANES) == 0
        in_specs.append(pl.BlockSpec((None, rows, c), lambda *g: (layer, step_of(*g), 0)))
        out_specs.append(pl.BlockSpec((rows, c), lambda *g: (step_of(*g), 0)))
        out_shape.append(jax.ShapeDtypeStruct((r, c), BF16))
    return in_specs, out_specs, out_shape


def _can_cast_in(stacks, n_steps):
    return all(s.shape[1] % (n_steps * 2 * SUBLANES) == 0 for s in stacks)


def _inproj_conv_kernel(*refs, tm, tiles_per_seq, seq_len, inject, n_cast):
    n_in = 8 if inject else 6
    cast_in = refs[n_in:n_in + n_cast]
    cast_out = refs[n_in + n_cast + 2:n_in + 2 * n_cast + 2]
    core = refs[:n_in] + refs[n_in + n_cast:n_in + n_cast + 2] + refs[n_in + 2 * n_cast + 2:]
    if inject:
        (x_ref, g_ref, wb_ref, wc_ref, wx_ref, wconv_ref, p1_ref, p2_ref,
         gated_ref, u_ref, a_s) = core
    else:
        (x_ref, g_ref, wb_ref, wc_ref, wx_ref, wconv_ref,
         gated_ref, u_ref, a_s, carry_s) = core
    i = pl.program_id(0)
    j = pl.program_id(1)
    for src, dst in zip(cast_in, cast_out):
        dst[...] = src[...].astype(BF16)

    @pl.when(j == 0)
    def _():
        a_s[...] = _rms(x_ref[...], g_ref[...]).astype(BF16)

    a = a_s[...]
    dot = functools.partial(jnp.dot, preferred_element_type=F32)
    u = dot(a, wc_ref[...]) * dot(a, wx_ref[...])
    rolled1 = pltpu.roll(u, 1, axis=0)
    rolled2 = pltpu.roll(u, 2, axis=0)
    row = lax.broadcasted_iota(jnp.int32, u.shape, 0)
    if inject:
        pos = row % seq_len
        u1 = jnp.where(pos >= 1, rolled1, p1_ref[...])
        u2 = jnp.where(pos >= 2, rolled2, p2_ref[...])
        u_ref[...] = u
    else:
        @pl.when(i % tiles_per_seq == 0)
        def _():
            carry_s[j] = jnp.zeros(carry_s.shape[1:], F32)

        tail = carry_s[j]
        t1 = tail[SUBLANES - 1:SUBLANES, :]
        t2 = tail[SUBLANES - 2:SUBLANES - 1, :]
        u1 = jnp.where(row == 0, t1, rolled1)
        u2 = jnp.where(row == 0, t2, jnp.where(row == 1, t1, rolled2))
        last = u[tm - SUBLANES:, :]
        carry_s[j] = last
        u_ref[...] = last
    wconv = wconv_ref[...]
    conv = wconv[0:1, :] * u2 + wconv[1:2, :] * u1 + wconv[2:3, :] * u
    gated_ref[...] = (dot(a, wb_ref[...]) * conv).astype(BF16)


def _inproj_conv(x, g, w_in, w_conv, *, tm, tn, seq_len, hist=None, cast=(), cast_layer=0):
    n, d = x.shape
    nj = d // tn
    c_in, c_out, c_shape = _cast_specs(cast, cast_layer, (n // tm) * nj, lambda i, j: i * nj + j)
    inject = hist is not None
    assert n % tm == 0 and d % tn == 0
    if not inject:
        assert seq_len % tm == 0
    x_spec = pl.BlockSpec((tm, d), lambda i, j: (i, 0))
    g_spec = pl.BlockSpec((1, d), lambda i, j: (0, 0))
    w_specs = [pl.BlockSpec((d, tn), lambda i, j, s=s: (0, j + s * nj)) for s in range(3)]
    wconv_spec = pl.BlockSpec((CONV_W, tn), lambda i, j: (0, j))
    tile_spec = pl.BlockSpec((tm, tn), lambda i, j: (i, j))
    in_specs = [x_spec, g_spec, *w_specs, wconv_spec]
    args = [x, g, w_in, w_in, w_in, w_conv]
    scratch = [pltpu.VMEM((tm, d), BF16)]
    if inject:
        in_specs += [tile_spec, tile_spec]
        args += list(hist)
        u_shape = jax.ShapeDtypeStruct((n, d), F32)
        u_spec = tile_spec
    else:
        scratch.append(pltpu.VMEM((nj, SUBLANES, tn), F32))
        u_shape = jax.ShapeDtypeStruct((n // tm * SUBLANES, d), F32)
        u_spec = pl.BlockSpec((SUBLANES, tn), lambda i, j: (i, j))
    kern = functools.partial(_inproj_conv_kernel, tm=tm, tiles_per_seq=max(seq_len // tm, 1),
                             seq_len=seq_len, inject=inject, n_cast=len(cast))
    return pl.pallas_call(
        kern,
        grid=(n // tm, nj),
        in_specs=in_specs + c_in,
        out_specs=[tile_spec, u_spec] + c_out,
        out_shape=[jax.ShapeDtypeStruct((n, d), BF16), u_shape] + c_shape,
        scratch_shapes=scratch,
        compiler_params=_params("arbitrary", "arbitrary"),
        name="inproj_conv",
    )(*args, *cast)


def _outproj_kernel(a_ref, w_ref, h_ref, gpost_ref, gnext_ref, h1_ref, anext_ref):
    mix = jnp.dot(a_ref[...], w_ref[...], preferred_element_type=F32)
    h1 = h_ref[...] + _rms(mix, gpost_ref[...])
    h1_ref[...] = h1
    anext_ref[...] = _rms(h1, gnext_ref[...]).astype(BF16)


def _outproj_residual(a, w, h, g_post, g_next, *, tm):
    n, d = h.shape
    row = pl.BlockSpec((tm, d), lambda i: (i, 0))
    vec = pl.BlockSpec((1, d), lambda i: (0, 0))
    return pl.pallas_call(
        _outproj_kernel,
        grid=(n // tm,),
        in_specs=[row, pl.BlockSpec((d, d), lambda i: (0, 0)), row, vec, vec],
        out_specs=[row, row],
        out_shape=[jax.ShapeDtypeStruct((n, d), F32), jax.ShapeDtypeStruct((n, d), BF16)],
        compiler_params=_params("arbitrary"),
        name="outproj_residual",
    )(a, w, h, g_post, g_next)


def _mlp_kernel(*refs, n_next, own_acc):
    a_ref, wup_ref, wdown_ref, h_ref, gpost_ref, gnext_ref, h2_ref = refs[:7]
    next_refs = refs[7:7 + n_next]
    acc = refs[7 + n_next] if own_acc else h2_ref
    f = pl.program_id(1)

    @pl.when(f == 0)
    def _():
        acc[...] = jnp.zeros(acc.shape, F32)

    hid = jnp.dot(a_ref[...], wup_ref[...], preferred_element_type=F32)
    hid = jnp.square(jnp.maximum(hid, 0.0)).astype(BF16)
    acc[...] += jnp.dot(hid, wdown_ref[...], preferred_element_type=F32)

    @pl.when(f == pl.num_programs(1) - 1)
    def _():
        h2 = h_ref[...] + _rms(acc[...], gpost_ref[...])
        h2_ref[...] = h2
        for k, ref in enumerate(next_refs):
            ref[...] = _rms(h2, gnext_ref[k:k + 1, :]).astype(BF16)


def _mlp_residual(a, w_up, w_down, h, g_post, g_next, *, tm, tf):
    n, d = h.shape
    d_ff = w_up.shape[1]
    n_next = 0 if g_next is None else g_next.shape[0]
    g_arr = jnp.ones((1, d), F32) if g_next is None else g_next
    own_acc = n_next > 0
    row = pl.BlockSpec((tm, d), lambda i, f: (i, 0))
    outs = [jax.ShapeDtypeStruct((n, d), F32)] + [jax.ShapeDtypeStruct((n, d), BF16)] * n_next
    return pl.pallas_call(
        functools.partial(_mlp_kernel, n_next=n_next, own_acc=own_acc),
        grid=(n // tm, d_ff // tf),
        in_specs=[row,
                  pl.BlockSpec((d, tf), lambda i, f: (0, f)),
                  pl.BlockSpec((tf, d), lambda i, f: (f, 0)),
                  row,
                  pl.BlockSpec((1, d), lambda i, f: (0, 0)),
                  pl.BlockSpec(g_arr.shape, lambda i, f: (0, 0))],
        out_specs=[row] * (1 + n_next),
        out_shape=outs,
        scratch_shapes=[pltpu.VMEM((tm, d), F32)] if own_acc else [],
        compiler_params=_params("arbitrary", "arbitrary"),
        name="mlp_residual",
    )(a, w_up, w_down, h, g_post, g_arr)


def _proj_kernel(a_ref, w_ref, *out_refs, scale):
    y = jnp.dot(a_ref[...], w_ref[...], preferred_element_type=F32)
    if scale is not None:
        y = y * scale
    for ref in out_refs:
        if len(ref.shape) == 3:
            per_head = [y[:, hh * HEAD_DIM:(hh + 1) * HEAD_DIM] for hh in range(ref.shape[1])]
            ref[...] = jnp.swapaxes(jnp.stack(per_head, axis=0), 0, 1).astype(ref.dtype)
        else:
            ref[...] = y.astype(ref.dtype)


def _proj(a, w, *, col0, n_out, out_dtypes, tm, tn, scale=None, split_heads=()):
    n, d = a.shape
    assert col0 % tn == 0 and n_out % tn == 0 and tn % HEAD_DIM == 0
    jb = col0 // tn
    tile = pl.BlockSpec((tm, tn), lambda i, j: (i, j))
    head_tile = pl.BlockSpec((tm, tn // HEAD_DIM, HEAD_DIM), lambda i, j: (i, j, 0))
    out_specs, out_shape = [], []
    for k, dt in enumerate(out_dtypes):
        if k in split_heads:
            out_specs.append(head_tile)
            out_shape.append(jax.ShapeDtypeStruct((n, n_out // HEAD_DIM, HEAD_DIM), dt))
        else:
            out_specs.append(tile)
            out_shape.append(jax.ShapeDtypeStruct((n, n_out), dt))
    return pl.pallas_call(
        functools.partial(_proj_kernel, scale=scale),
        grid=(n // tm, n_out // tn),
        in_specs=[pl.BlockSpec((tm, d), lambda i, j: (i, 0)),
                  pl.BlockSpec((d, tn), lambda i, j: (0, j + jb))],
        out_specs=out_specs,
        out_shape=out_shape,
        compiler_params=_params("arbitrary", "arbitrary"),
        name="proj",
    )(a, w)


def _logf_kernel(a_ref, wf_ref, bf_ref, lf_ref, c_ref, carry_s, *, tm, tiles_per_seq, seq_len):
    i = pl.program_id(0)
    x = jnp.dot(a_ref[...], wf_ref[...], preferred_element_type=F32) + bf_ref[...]
    lf = jnp.minimum(x, 0.0) - jnp.log1p(jnp.exp(-jnp.abs(x)))
    lf_ref[...] = lf
    r = lax.broadcasted_iota(jnp.int32, (tm, tm), 0)
    c = lax.broadcasted_iota(jnp.int32, (tm, tm), 1)
    same_seq = (r // seq_len) == (c // seq_len) if seq_len < tm else (r >= 0)
    tri = jnp.where((c <= r) & same_seq, 1.0, 0.0).astype(BF16)
    csum = _dot01_exact(tri, lf)

    @pl.when(i % tiles_per_seq == 0)
    def _():
        carry_s[...] = jnp.zeros(carry_s.shape, F32)

    csum = csum + carry_s[0:1, :]
    c_ref[...] = csum
    carry_s[...] = jnp.broadcast_to(csum[tm - 1:tm, :], carry_s.shape)


def _logf(a, w_f, b_f, *, tm, seq_len):
    n, d = a.shape
    assert seq_len % tm == 0 or tm % seq_len == 0
    tile = pl.BlockSpec((tm, LANES), lambda i: (i, 0))
    kern = functools.partial(_logf_kernel, tm=tm, tiles_per_seq=max(seq_len // tm, 1), seq_len=seq_len)
    return pl.pallas_call(
        kern,
        grid=(n // tm,),
        in_specs=[pl.BlockSpec((tm, d), lambda i: (i, 0)),
                  pl.BlockSpec((d, LANES), lambda i: (0, 0)),
                  pl.BlockSpec((1, LANES), lambda i: (0, 0))],
        out_specs=[tile, tile],
        out_shape=[jax.ShapeDtypeStruct((n, LANES), F32)] * 2,
        scratch_shapes=[pltpu.VMEM((SUBLANES, LANES), F32)],
        compiler_params=_params("arbitrary"),
        name="logf",
    )(a, w_f, b_f)


def _online_softmax_update(chunks, cq, m_old, l_old):
    mx = functools.reduce(jnp.maximum, chunks)
    rowmax = jnp.broadcast_to(jnp.max(mx, axis=-1, keepdims=True), mx.shape)
    m_new = jnp.maximum(m_old, rowmax + cq)
    alpha = jnp.exp2(m_old - m_new)
    shift = m_new - cq
    p = [jnp.exp2(c - shift) for c in chunks]
    rowsum = jnp.broadcast_to(jnp.sum(functools.reduce(jnp.add, p), axis=-1, keepdims=True), mx.shape)
    return m_new, alpha * l_old + rowsum, alpha, p


def _fox_prompt_kernel(q_ref, k_ref, v_ref, cq_ref, ck_ref, *refs, tq, tk, n_cast):
    cast_in = refs[:n_cast]
    o_ref = refs[n_cast]
    cast_out = refs[n_cast + 1:2 * n_cast + 1]
    m_s, l_s, acc_s, cqb_s = refs[2 * n_cast + 1:]
    qi = pl.program_id(1)
    ki = pl.program_id(2)
    n_chunks = tk // LANES
    for src, dst in zip(cast_in, cast_out):
        dst[...] = src[...].astype(BF16)

    @pl.when(ki == 0)
    def _():
        m_s[...] = jnp.full(m_s.shape, -jnp.inf, F32)
        l_s[...] = jnp.zeros(l_s.shape, F32)
        acc_s[...] = jnp.zeros(acc_s.shape, F32)
        cq = cq_ref[0]
        for h in range(N_HEADS):
            cqb_s[h] = jnp.broadcast_to(cq[:, h:h + 1], cqb_s.shape[1:])

    def step(on_diagonal):
        ck = ck_ref[0]
        if on_diagonal:
            row = lax.broadcasted_iota(jnp.int32, (tq, LANES), 0)
            lane = lax.broadcasted_iota(jnp.int32, (tq, LANES), 1)
        for h in range(N_HEADS):
            cols = slice(h * HEAD_DIM, (h + 1) * HEAD_DIM)
            s = lax.dot_general(q_ref[0, :, cols], k_ref[0, :, cols],
                                (((1,), (1,)), ((), ())), preferred_element_type=F32)
            chunks = []
            for c in range(n_chunks):
                lanes = slice(c * LANES, (c + 1) * LANES)
                sc = s[:, lanes] - ck[h:h + 1, lanes]
                if on_diagonal:
                    sc = jnp.where(lane + c * LANES <= row, sc, NEG_BIG)
                chunks.append(sc)
            m_new, l_new, alpha, p = _online_softmax_update(chunks, cqb_s[h], m_s[h], l_s[h])
            m_s[h] = m_new
            l_s[h] = l_new
            pv = jnp.dot(jnp.concatenate(p, axis=1).astype(BF16), v_ref[0, :, cols],
                         preferred_element_type=F32)
            acc_s[:, cols] = alpha * acc_s[:, cols] + pv

    pl.when(ki < qi)(functools.partial(step, False))
    pl.when(ki == qi)(functools.partial(step, True))

    @pl.when(ki == pl.num_programs(2) - 1)
    def _():
        for h in range(N_HEADS):
            cols = slice(h * HEAD_DIM, (h + 1) * HEAD_DIM)
            o_ref[0, :, cols] = (acc_s[:, cols] / l_s[h]).astype(o_ref.dtype)


def _fox_prompt(q, k, v, c_rows, c_heads, *, tq, tk, cast=(), cast_layer=0):
    b, t, d = q.shape
    assert tq == tk
    nq, nk = t // tq, t // tk
    c_in, c_out, c_shape = _cast_specs(cast, cast_layer, b * nq * nk,
                                       lambda bi, qi, ki: (bi * nq + qi) * nk + ki)
    kv_map = lambda bi, qi, ki: (bi, jnp.minimum(ki, qi), 0)
    return pl.pallas_call(
        functools.partial(_fox_prompt_kernel, tq=tq, tk=tk, n_cast=len(cast)),
        grid=(b, nq, nk),
        in_specs=[pl.BlockSpec((1, tq, d), lambda bi, qi, ki: (bi, qi, 0)),
                  pl.BlockSpec((1, tk, d), kv_map),
                  pl.BlockSpec((1, tk, d), kv_map),
                  pl.BlockSpec((1, tq, LANES), lambda bi, qi, ki: (bi, qi, 0)),
                  pl.BlockSpec((1, N_HEADS, tk), lambda bi, qi, ki: (bi, 0, jnp.minimum(ki, qi))),
                  *c_in],
        out_specs=[pl.BlockSpec((1, tq, d), lambda bi, qi, ki: (bi, qi, 0)), *c_out],
        out_shape=[jax.ShapeDtypeStruct((b, t, d), BF16), *c_shape],
        scratch_shapes=[pltpu.VMEM((N_HEADS, tq, LANES), F32),
                        pltpu.VMEM((N_HEADS, tq, LANES), F32),
                        pltpu.VMEM((tq, d), F32),
                        pltpu.VMEM((N_HEADS, tq, LANES), F32)],
        compiler_params=_params("arbitrary", "arbitrary", "arbitrary"),
        name="fox_prompt",
    )(q, k, v, c_rows, c_heads, *cast)


def _page_suffix_kernel(lf_ref, out_ref, *, heads):
    x = lf_ref[...]
    width = x.shape[1]
    lane = lax.broadcasted_iota(jnp.int32, x.shape, 1)
    incl = x
    k = heads
    while k < width:
        incl = incl + jnp.where(lane < width - k, pltpu.roll(incl, width - k, axis=1), 0.0)
        k *= 2
    total = jnp.where(lane < heads, incl, 0.0)
    k = heads
    while k < width:
        total = total + pltpu.roll(total, k, axis=1)
        k *= 2
    out_ref[:, 0, :] = (incl - x) * LOG2E
    out_ref[:, 1, :] = total * LOG2E


def _page_suffix(logf_flat, *, heads, pages_per_step):
    n_pool, width = logf_flat.shape
    assert n_pool % pages_per_step == 0
    return pl.pallas_call(
        functools.partial(_page_suffix_kernel, heads=heads),
        grid=(n_pool // pages_per_step,),
        in_specs=[pl.BlockSpec((pages_per_step, width), lambda i: (i, 0))],
        out_specs=pl.BlockSpec((pages_per_step, 2, width), lambda i: (i, 0, 0)),
        out_shape=jax.ShapeDtypeStruct((n_pool, 2, width), F32),
        compiler_params=_params("arbitrary"),
        name="page_suffix",
    )(logf_flat)


def _fox_sample_step(step, n_steps, q_ref, kn_ref, vn_ref, cq_ref, cnrow_ref, k_refs, v_refs, s_refs,
                     o_ref, m_s, l_s, acc_s, carry_s, *, joint, t_new):
    group = len(k_refs)
    rows = q_ref.shape[1]
    heads = rows // t_new
    width = carry_s.shape[1]
    nt = (((1,), (1,)), ((), ()))
    q = q_ref[0]
    cq = cq_ref[0]

    def own_head(n_lanes):
        lane_head = lax.broadcasted_iota(jnp.int32, (t_new, n_lanes), 1) % heads
        return [lane_head == h for h in range(heads)]

    def compact(s, masks):
        out = s[(heads - 1) * t_new:, :]
        for h in range(heads - 2, -1, -1):
            out = jnp.where(masks[h], s[h * t_new:(h + 1) * t_new, :], out)
        return out

    def expand(p, masks):
        return jnp.concatenate([jnp.where(masks[h], p, 0.0) for h in range(heads)], axis=0)

    def per_head_rows(x):
        return jnp.concatenate([jnp.broadcast_to(x[:, h:h + 1], (t_new, LANES)) for h in range(heads)],
                               axis=0)

    def over_positions(x, op):
        k = heads
        while k < LANES:
            x = op(x, pltpu.roll(x, k, axis=1))
            k *= 2
        return x

    def update(blocks):
        tiles = [c[:, i * LANES:(i + 1) * LANES] for c, _, _ in blocks for i in range(c.shape[1] // LANES)]
        m_old = m_s[...]
        m_new = jnp.maximum(m_old, over_positions(functools.reduce(jnp.maximum, tiles), jnp.maximum) + cq)
        alpha = jnp.exp2(m_old - m_new)
        shift = m_new - cq
        acc = per_head_rows(alpha) * acc_s[...]
        l_add = jnp.zeros_like(m_old)
        start = 0
        for c, v, masks in blocks:
            n = c.shape[1] // LANES
            p_tiles = [jnp.exp2(t - shift) for t in tiles[start:start + n]]
            start += n
            l_add = l_add + functools.reduce(jnp.add, p_tiles)
            acc = acc + jnp.dot(expand(jnp.concatenate(p_tiles, axis=1), masks), v,
                                preferred_element_type=F32)
        m_s[...] = m_new
        l_s[...] = alpha * l_s[...] + over_positions(l_add, jnp.add)
        acc_s[...] = acc

    @pl.when(step == 0)
    def _():
        m_s[...] = jnp.full(m_s.shape, -jnp.inf, F32)
        l_s[...] = jnp.zeros(l_s.shape, F32)
        acc_s[...] = jnp.zeros(acc_s.shape, F32)
        carry_s[...] = jnp.zeros(carry_s.shape, F32)
        masks = own_head(LANES)
        s = compact(lax.dot_general(q, kn_ref[0], nt, preferred_element_type=F32), masks) - cnrow_ref[0]
        t_q = lax.broadcasted_iota(jnp.int32, s.shape, 0)
        t_k = lax.broadcasted_iota(jnp.int32, s.shape, 1) // heads
        update([(jnp.where(t_k <= t_q, s, NEG_BIG), vn_ref[0], masks)])

    masks = own_head(width)
    carry = carry_s[...]
    for g0 in range(0, group, joint):
        blocks = []
        for g in range(g0, g0 + joint):
            sfx = s_refs[g][0]
            kf = k_refs[g][0].reshape(width, HEAD_DIM)
            vf = v_refs[g][0].reshape(width, HEAD_DIM)
            s = compact(lax.dot_general(q, kf, nt, preferred_element_type=F32), masks)
            blocks.append((s + (carry + sfx[0:1, :]), vf, masks))
            carry = carry + sfx[1:2, :]
        update(blocks)
    carry_s[...] = carry

    @pl.when(step == n_steps - 1)
    def _():
        o_ref[0] = (acc_s[...] / per_head_rows(l_s[...])).astype(o_ref.dtype)


def _fox_sample_kernel(pt_ref, q_ref, kn_ref, vn_ref, cq_ref, cnrow_ref, *refs, group, joint, t_new):
    k_refs = refs[:group]
    v_refs = refs[group:2 * group]
    s_refs = refs[2 * group:3 * group]
    o_ref = refs[3 * group]
    _fox_sample_step(pl.program_id(1), pl.num_programs(1), q_ref, kn_ref, vn_ref, cq_ref, cnrow_ref,
                     k_refs, v_refs, s_refs, o_ref, *refs[3 * group + 1:], joint=joint, t_new=t_new)


def _fox_sample(page_table, q, k_new, v_new, cq_tile, cn_row, cache_k, cache_v, page_sfx,
                *, group, joint, t_new):
    b, rows, dh = q.shape
    n_pages = page_table.shape[1]
    _, page, heads, _ = cache_k.shape
    width = page * heads
    assert n_pages % group == 0 and group % joint == 0 and rows == heads * t_new and rows == LANES
    assert LANES % heads == 0

    def page_map(g, ndim):
        return lambda bi, si, pt: (pt[bi, n_pages - 1 - (si * group + g)],) + (0,) * (ndim - 1)

    per_seq = lambda shape: pl.BlockSpec((1,) + shape, lambda bi, si, pt: (bi, 0, 0))
    kv_specs = [pl.BlockSpec((1, page, heads, dh), page_map(g, 4)) for g in range(group)]
    sfx_specs = [pl.BlockSpec((1, 2, width), page_map(g, 3)) for g in range(group)]
    stat = pltpu.VMEM((t_new, LANES), F32)
    grid_spec = pltpu.PrefetchScalarGridSpec(
        num_scalar_prefetch=1,
        grid=(b, n_pages // group),
        in_specs=[per_seq((rows, dh)), per_seq((rows, dh)), per_seq((rows, dh)),
                  per_seq((t_new, LANES)), per_seq((1, rows)),
                  *kv_specs, *kv_specs, *sfx_specs],
        out_specs=per_seq((rows, dh)),
        scratch_shapes=[stat, stat, pltpu.VMEM((rows, dh), F32), pltpu.VMEM((1, width), F32)],
    )
    return pl.pallas_call(
        functools.partial(_fox_sample_kernel, group=group, joint=joint, t_new=t_new),
        grid_spec=grid_spec,
        out_shape=jax.ShapeDtypeStruct((b, rows, dh), BF16),
        compiler_params=_params("arbitrary", "arbitrary"),
        name="fox_sample",
    )(page_table, q, k_new, v_new, cq_tile, cn_row,
      *([cache_k] * group), *([cache_v] * group), *([page_sfx] * group))


def _layer0_and_kv(x, seq_len, hist, kv16, q_dtype, w, mlp_w, *, tm):
    n, d = x.shape
    tn = 512
    cast = ()
    if mlp_w is None:
        if _can_cast_in(w["mlp_f32"], (n // tm) * (d // tn)):
            cast = w["mlp_f32"]
        else:
            mlp_w = tuple(s[0].astype(BF16) for s in w["mlp_f32"])
    gated, u_rows, *cast_out = _inproj_conv(x, w["g_mix_pre"][0:1], w["w_in"], w["w_conv"], tm=tm, tn=tn,
                                            seq_len=seq_len, hist=hist, cast=cast, cast_layer=0)
    if cast:
        mlp_w = tuple(cast_out)
    row_tm = min(n, 512)
    h, a = _outproj_residual(gated, w["w_out"], x, w["g_mix_post"][0:1], w["g_ffn_pre"][0:1], tm=row_tm)
    g_next = jnp.concatenate([w["g_kv"], w["g_mix_pre"][1:2]], axis=0)
    h, a_kv, a_q = _mlp_residual(a, *mlp_w, h, w["g_ffn_post"][0:1], g_next, tm=row_tm, tf=512)
    ptm = min(n, 1024)
    kv_dtypes = (F32, BF16) if kv16 else (F32,)
    k = _proj(a_kv, w["w_kvf"], col0=0, n_out=d, out_dtypes=kv_dtypes, tm=ptm, tn=1024, split_heads=(0,))
    v = _proj(a_kv, w["w_kvf"], col0=d, n_out=d, out_dtypes=kv_dtypes, tm=ptm, tn=1024, split_heads=(0,))
    lf, c = _logf(a_kv, w["w_f"], w["b_f"], tm=min(n, 256), seq_len=seq_len)
    (q,) = _proj(a_q, w["w_q"], col0=0, n_out=d, out_dtypes=(q_dtype,), tm=ptm, tn=1024,
                 scale=LOG2E * HEAD_DIM ** -0.5)
    return h, u_rows, k, v, lf, q, c * LOG2E, mlp_w


def kernel(x_prompt, x_sample, state_conv, cache_k, cache_v, cache_logf, page_table,
           w_in_a, w_conv_a, w_out_a, g_kv, w_kvf, b_f, w_q, w_o,
           g_mix_pre, g_mix_post, g_ffn_pre, g_ffn_post, w_up, w_down):
    bp, t, d = x_prompt.shape
    bs, ts, _ = x_sample.shape
    heads = N_HEADS
    w = {
        "w_in": w_in_a[0].astype(BF16), "w_conv": w_conv_a[0], "w_out": w_out_a[0].astype(BF16),
        "g_kv": g_kv[None, :], "w_kvf": w_kvf.astype(BF16),
        "w_f": jnp.pad(w_kvf[:, 2 * d:], ((0, 0), (0, LANES - heads))).astype(BF16),
        "b_f": jnp.pad(b_f, (0, LANES - heads))[None, :],
        "w_q": w_q[0].astype(BF16), "w_o": w_o[0].astype(BF16),
        "g_mix_pre": g_mix_pre, "g_mix_post": g_mix_post,
        "g_ffn_pre": g_ffn_pre, "g_ffn_post": g_ffn_post,
        "mlp_f32": (w_up, w_down),
    }

    n_p = bp * t
    tm_p = min(t, 1024)
    tq = 512
    h_p, u_p, (k_p, k16), (v_p, v16), lf_p, q_p, c_p, mlp0 = _layer0_and_kv(
        x_prompt.reshape(n_p, d), t, None, True, BF16, w, None, tm=tm_p)
    conv_prompt = u_p.reshape(bp, t // tm_p, SUBLANES, d)[:, -1, SUBLANES - (CONV_W - 1):, :][None]
    c3 = c_p.reshape(bp, t, LANES)
    cast1 = w["mlp_f32"] if _can_cast_in(w["mlp_f32"], bp * (t // tq) ** 2) else ()
    o_p, *mlp1 = _fox_prompt(q_p.reshape(bp, t, d), k16.reshape(bp, t, d), v16.reshape(bp, t, d),
                             c3, jnp.swapaxes(c3[:, :, :heads], 1, 2), tq=tq, tk=tq,
                             cast=cast1, cast_layer=1)
    if not cast1:
        mlp1 = [s[1].astype(BF16) for s in w["mlp_f32"]]
    h_p, a_p = _outproj_residual(o_p.reshape(n_p, d), w["w_o"], h_p, w["g_mix_post"][1:2],
                                 w["g_ffn_pre"][1:2], tm=512)
    (y_p,) = _mlp_residual(a_p, *mlp1, h_p, w["g_ffn_post"][1:2], None, tm=min(n_p, 1024), tf=512)

    n_s = bs * ts
    prev = state_conv[0]
    zeros = jnp.zeros((bs, ts - 2, d), F32)
    p1 = jnp.concatenate([prev[:, 1:2], zeros, zeros[:, :1]], axis=1).reshape(n_s, d)
    p2 = jnp.concatenate([prev, zeros], axis=1).reshape(n_s, d)
    h_s, u_s, (k_s,), (v_s,), lf_s, q_s, c_s, _ = _layer0_and_kv(
        x_sample.reshape(n_s, d), ts, (p1, p2), False, F32, w, mlp0, tm=n_s)
    conv_sample = u_s.reshape(bs, ts, d)[:, ts - (CONV_W - 1):, :][None]
    n_pool, page = cache_logf.shape[:2]
    page_sfx = _page_suffix(cache_logf.reshape(n_pool, page * heads), heads=heads,
                            pages_per_step=64 if n_pool % 64 == 0 else n_pool)
    q_rows = q_s.reshape(bs, ts, heads, HEAD_DIM).transpose(0, 2, 1, 3).reshape(bs, heads * ts, HEAD_DIM)
    cn = c_s[:, :heads].reshape(bs, ts, heads)
    o_s = _fox_sample(page_table, q_rows,
                      k_s.reshape(bs, ts * heads, HEAD_DIM), v_s.reshape(bs, ts * heads, HEAD_DIM),
                      jnp.tile(cn, (1, 1, LANES // heads)), cn.reshape(bs, 1, ts * heads),
                      cache_k, cache_v, page_sfx, group=8, joint=SAMPLE_JOINT, t_new=ts)
    o_s = o_s.reshape(bs, heads, ts, HEAD_DIM).transpose(0, 2, 1, 3).reshape(n_s, d)
    h_s, a_s = _outproj_residual(o_s, w["w_o"], h_s, w["g_mix_post"][1:2], w["g_ffn_pre"][1:2], tm=n_s)
    (y_s,) = _mlp_residual(a_s, *mlp1, h_s, w["g_ffn_post"][1:2], None, tm=n_s, tf=512)

    return (y_p.reshape(bp, t, d), y_s.reshape(bs, ts, d), conv_prompt, conv_sample,
            k_p.reshape(bp, t, heads, HEAD_DIM), v_p.reshape(bp, t, heads, HEAD_DIM),
            lf_p[:, :heads].reshape(bp, t, heads),
            k_s.reshape(bs, ts, heads, HEAD_DIM), v_s.reshape(bs, ts, heads, HEAD_DIM),
            lf_s[:, :heads].reshape(bs, ts, heads))
```

```python
import functools
import math

import jax
import jax.numpy as jnp
from jax import lax
from jax.experimental import pallas as pl
from jax.experimental.pallas import tpu as pltpu

RMS_EPS = 1e-6
CONV_W = 3
N_HEADS = 16
HEAD_DIM = 128
LANES = 128
SUBLANES = 8
VMEM_LIMIT = 60 * 1024 * 1024
NEG_BIG = -1e30
LOG2E = math.log2(math.e)
SAMPLE_JOINT = 4

F32 = jnp.float32
BF16 = jnp.bfloat16


def _params(*sem):
    return pltpu.CompilerParams(dimension_semantics=sem, vmem_limit_bytes=VMEM_LIMIT)


def _rms(x, g):
    return x * lax.rsqrt(jnp.mean(x * x, axis=-1, keepdims=True) + RMS_EPS) * g


def _split3(x):
    a = x.astype(BF16)
    r = x - a.astype(F32)
    b = r.astype(BF16)
    c = (r - b.astype(F32)).astype(BF16)
    return a, b, c


def _dot01_exact(m01, x):
    a, b, c = _split3(x)
    d = functools.partial(jnp.dot, preferred_element_type=F32)
    return d(m01, a) + d(m01, b) + d(m01, c)


def _cast_specs(stacks, layer, n_steps, step_of):
    in_specs, out_specs, out_shape = [], [], []
    for stack in stacks:
        _, r, c = stack.shape
        rows = r // n_steps
        assert rows * n_steps == r and rows % (2 * SUBLANES) == 0
        in_specs.append(pl.BlockSpec((None, rows, c), lambda *g: (layer, step_of(*g), 0)))
        out_specs.append(pl.BlockSpec((rows, c), lambda *g: (step_of(*g), 0)))
        out_shape.append(jax.ShapeDtypeStruct((r, c), BF16))
    return in_specs, out_specs, out_shape


def _can_cast_in(stacks, n_steps):
    return all(s.shape[1] % (n_steps * 2 * SUBLANES) == 0 for s in stacks)


def _inproj_conv_kernel(*refs, tm, tiles_per_seq, seq_len, inject, n_cast):
    n_in = 8 if inject else 6
    cast_in = refs[n_in:n_in + n_cast]
    cast_out = refs[n_in + n_cast + 2:n_in + 2 * n_cast + 2]
    core = refs[:n_in] + refs[n_in + n_cast:n_in + n_cast + 2] + refs[n_in + 2 * n_cast + 2:]
    if inject:
        (x_ref, g_ref, wb_ref, wc_ref, wx_ref, wconv_ref, p1_ref, p2_ref,
         gated_ref, u_ref, a_s) = core
    else:
        (x_ref, g_ref, wb_ref, wc_ref, wx_ref, wconv_ref,
         gated_ref, u_ref, a_s, carry_s) = core
    i = pl.program_id(0)
    j = pl.program_id(1)
    for src, dst in zip(cast_in, cast_out):
        dst[...] = src[...].astype(BF16)

    @pl.when(j == 0)
    def _():
        a_s[...] = _rms(x_ref[...], g_ref[...]).astype(BF16)

    a = a_s[...]
    dot = functools.partial(jnp.dot, preferred_element_type=F32)
    u = dot(a, wc_ref[...]) * dot(a, wx_ref[...])
    rolled1 = pltpu.roll(u, 1, axis=0)
    rolled2 = pltpu.roll(u, 2, axis=0)
    row = lax.broadcasted_iota(jnp.int32, u.shape, 0)
    if inject:
        pos = row % seq_len
        u1 = jnp.where(pos >= 1, rolled1, p1_ref[...])
        u2 = jnp.where(pos >= 2, rolled2, p2_ref[...])
        u_ref[...] = u
    else:
        @pl.when(i % tiles_per_seq == 0)
        def _():
            carry_s[j] = jnp.zeros(carry_s.shape[1:], F32)

        tail = carry_s[j]
        t1 = tail[SUBLANES - 1:SUBLANES, :]
        t2 = tail[SUBLANES - 2:SUBLANES - 1, :]
        u1 = jnp.where(row == 0, t1, rolled1)
        u2 = jnp.where(row == 0, t2, jnp.where(row == 1, t1, rolled2))
        last = u[tm - SUBLANES:, :]
        carry_s[j] = last
        u_ref[...] = last
    wconv = wconv_ref[...]
    conv = wconv[0:1, :] * u2 + wconv[1:2, :] * u1 + wconv[2:3, :] * u
    gated_ref[...] = (dot(a, wb_ref[...]) * conv).astype(BF16)


def _inproj_conv(x, g, w_in, w_conv, *, tm, tn, seq_len, hist=None, cast=(), cast_layer=0):
    n, d = x.shape
    nj = d // tn
    c_in, c_out, c_shape = _cast_specs(cast, cast_layer, (n // tm) * nj, lambda i, j: i * nj + j)
    inject = hist is not None
    assert n % tm == 0 and d % tn == 0
    if not inject:
        assert seq_len % tm == 0
    x_spec = pl.BlockSpec((tm, d), lambda i, j: (i, 0))
    g_spec = pl.BlockSpec((1, d), lambda i, j: (0, 0))
    w_specs = [pl.BlockSpec((d, tn), lambda i, j, s=s: (0, j + s * nj)) for s in range(3)]
    wconv_spec = pl.BlockSpec((CONV_W, tn), lambda i, j: (0, j))
    tile_spec = pl.BlockSpec((tm, tn), lambda i, j: (i, j))
    in_specs = [x_spec, g_spec, *w_specs, wconv_spec]
    args = [x, g, w_in, w_in, w_in, w_conv]
    scratch = [pltpu.VMEM((tm, d), BF16)]
    if inject:
        in_specs += [tile_spec, tile_spec]
        args += list(hist)
        u_shape = jax.ShapeDtypeStruct((n, d), F32)
        u_spec = tile_spec
    else:
        scratch.append(pltpu.VMEM((nj, SUBLANES, tn), F32))
        u_shape = jax.ShapeDtypeStruct((n // tm * SUBLANES, d), F32)
        u_spec = pl.BlockSpec((SUBLANES, tn), lambda i, j: (i, j))
    kern = functools.partial(_inproj_conv_kernel, tm=tm, tiles_per_seq=max(seq_len // tm, 1),
                             seq_len=seq_len, inject=inject, n_cast=len(cast))
    return pl.pallas_call(
        kern,
        grid=(n // tm, nj),
        in_specs=in_specs + c_in,
        out_specs=[tile_spec, u_spec] + c_out,
        out_shape=[jax.ShapeDtypeStruct((n, d), BF16), u_shape] + c_shape,
        scratch_shapes=scratch,
        compiler_params=_params("arbitrary", "arbitrary"),
        name="inproj_conv",
    )(*args, *cast)


def _outproj_kernel(a_ref, w_ref, h_ref, gpost_ref, gnext_ref, h1_ref, anext_ref):
    mix = jnp.dot(a_ref[...], w_ref[...], preferred_element_type=F32)
    h1 = h_ref[...] + _rms(mix, gpost_ref[...])
    h1_ref[...] = h1
    anext_ref[...] = _rms(h1, gnext_ref[...]).astype(BF16)


def _outproj_residual(a, w, h, g_post, g_next, *, tm):
    n, d = h.shape
    row = pl.BlockSpec((tm, d), lambda i: (i, 0))
    vec = pl.BlockSpec((1, d), lambda i: (0, 0))
    return pl.pallas_call(
        _outproj_kernel,
        grid=(n // tm,),
        in_specs=[row, pl.BlockSpec((d, d), lambda i: (0, 0)), row, vec, vec],
        out_specs=[row, row],
        out_shape=[jax.ShapeDtypeStruct((n, d), F32), jax.ShapeDtypeStruct((n, d), BF16)],
        compiler_params=_params("arbitrary"),
        name="outproj_residual",
    )(a, w, h, g_post, g_next)


def _mlp_kernel(*refs, n_next, own_acc, n_cast):
    a_ref, wup_ref, wdown_ref, h_ref, gpost_ref, gnext_ref = refs[:6]
    cast_in = refs[6:6 + n_cast]
    h2_ref = refs[6 + n_cast]
    next_refs = refs[7 + n_cast:7 + n_cast + n_next]
    cast_out = refs[7 + n_cast + n_next:7 + 2 * n_cast + n_next]
    acc = refs[7 + 2 * n_cast + n_next] if own_acc else h2_ref
    f = pl.program_id(1)

    @pl.when(f == 0)
    def _():
        acc[...] = jnp.zeros(acc.shape, F32)
        for src, dst in zip(cast_in, cast_out):
            dst[...] = src[...].astype(BF16)

    hid = jnp.dot(a_ref[...], wup_ref[...], preferred_element_type=F32)
    hid = jnp.square(jnp.maximum(hid, 0.0)).astype(BF16)
    acc[...] += jnp.dot(hid, wdown_ref[...], preferred_element_type=F32)

    @pl.when(f == pl.num_programs(1) - 1)
    def _():
        h2 = h_ref[...] + _rms(acc[...], gpost_ref[...])
        h2_ref[...] = h2
        for k, ref in enumerate(next_refs):
            ref[...] = _rms(h2, gnext_ref[k:k + 1, :]).astype(BF16)


def _mlp_residual(a, w_up, w_down, h, g_post, g_next, *, tm, tf, cast=()):
    n, d = h.shape
    d_ff = w_up.shape[1]
    n_next = 0 if g_next is None else g_next.shape[0]
    g_arr = jnp.ones((1, d), F32) if g_next is None else g_next
    own_acc = n_next > 0
    c_in, c_out, c_shape = _cast_specs(cast, 0, n // tm, lambda i, f: i)
    row = pl.BlockSpec((tm, d), lambda i, f: (i, 0))
    outs = [jax.ShapeDtypeStruct((n, d), F32)] + [jax.ShapeDtypeStruct((n, d), BF16)] * n_next
    return pl.pallas_call(
        functools.partial(_mlp_kernel, n_next=n_next, own_acc=own_acc, n_cast=len(cast)),
        grid=(n // tm, d_ff // tf),
        in_specs=[row,
                  pl.BlockSpec((d, tf), lambda i, f: (0, f)),
                  pl.BlockSpec((tf, d), lambda i, f: (f, 0)),
                  row,
                  pl.BlockSpec((1, d), lambda i, f: (0, 0)),
                  pl.BlockSpec(g_arr.shape, lambda i, f: (0, 0)),
                  *c_in],
        out_specs=[row] * (1 + n_next) + c_out,
        out_shape=outs + c_shape,
        scratch_shapes=[pltpu.VMEM((tm, d), F32)] if own_acc else [],
        compiler_params=_params("arbitrary", "arbitrary"),
        name="mlp_residual",
    )(a, w_up, w_down, h, g_post, g_arr, *cast)


def _proj_kernel(a_ref, w_ref, *refs, scale, n_cast):
    cast_in = refs[:n_cast]
    out_refs = refs[n_cast:len(refs) - n_cast]
    for src, dst in zip(cast_in, refs[len(refs) - n_cast:]):
        dst[...] = src[...].astype(BF16)
    y = jnp.dot(a_ref[...], w_ref[...], preferred_element_type=F32)
    if scale is not None:
        y = y * scale
    for ref in out_refs:
        if len(ref.shape) == 3:
            per_head = [y[:, hh * HEAD_DIM:(hh + 1) * HEAD_DIM] for hh in range(ref.shape[1])]
            ref[...] = jnp.swapaxes(jnp.stack(per_head, axis=0), 0, 1).astype(ref.dtype)
        else:
            ref[...] = y.astype(ref.dtype)


def _proj(a, w, *, col0, n_out, out_dtypes, tm, tn, scale=None, split_heads=(), cast=(), cast_layer=0):
    n, d = a.shape
    assert col0 % tn == 0 and n_out % tn == 0 and tn % HEAD_DIM == 0
    jb = col0 // tn
    nj = n_out // tn
    c_in, c_out, c_shape = _cast_specs(cast, cast_layer, (n // tm) * nj, lambda i, j: i * nj + j)
    tile = pl.BlockSpec((tm, tn), lambda i, j: (i, j))
    head_tile = pl.BlockSpec((tm, tn // HEAD_DIM, HEAD_DIM), lambda i, j: (i, j, 0))
    out_specs, out_shape = [], []
    for k, dt in enumerate(out_dtypes):
        if k in split_heads:
            out_specs.append(head_tile)
            out_shape.append(jax.ShapeDtypeStruct((n, n_out // HEAD_DIM, HEAD_DIM), dt))
        else:
            out_specs.append(tile)
            out_shape.append(jax.ShapeDtypeStruct((n, n_out), dt))
    return pl.pallas_call(
        functools.partial(_proj_kernel, scale=scale, n_cast=len(cast)),
        grid=(n // tm, nj),
        in_specs=[pl.BlockSpec((tm, d), lambda i, j: (i, 0)),
                  pl.BlockSpec((d, tn), lambda i, j: (0, j + jb)),
                  *c_in],
        out_specs=out_specs + c_out,
        out_shape=out_shape + c_shape,
        compiler_params=_params("arbitrary", "arbitrary"),
        name="proj",
    )(a, w, *cast)


def _logf_kernel(a_ref, wf_ref, bf_ref, lf_ref, c_ref, carry_s, *, tm, tiles_per_seq, seq_len):
    i = pl.program_id(0)
    x = jnp.dot(a_ref[...], wf_ref[...], preferred_element_type=F32) + bf_ref[...]
    lf = jnp.minimum(x, 0.0) - jnp.log1p(jnp.exp(-jnp.abs(x)))
    lf_ref[...] = lf
    r = lax.broadcasted_iota(jnp.int32, (tm, tm), 0)
    c = lax.broadcasted_iota(jnp.int32, (tm, tm), 1)
    same_seq = (r // seq_len) == (c // seq_len) if seq_len < tm else (r >= 0)
    tri = jnp.where((c <= r) & same_seq, 1.0, 0.0).astype(BF16)
    csum = _dot01_exact(tri, lf)

    @pl.when(i % tiles_per_seq == 0)
    def _():
        carry_s[...] = jnp.zeros(carry_s.shape, F32)

    csum = csum + carry_s[0:1, :]
    c_ref[...] = csum
    carry_s[...] = jnp.broadcast_to(csum[tm - 1:tm, :], carry_s.shape)


def _logf(a, w_f, b_f, *, tm, seq_len):
    n, d = a.shape
    assert seq_len % tm == 0 or tm % seq_len == 0
    tile = pl.BlockSpec((tm, LANES), lambda i: (i, 0))
    kern = functools.partial(_logf_kernel, tm=tm, tiles_per_seq=max(seq_len // tm, 1), seq_len=seq_len)
    return pl.pallas_call(
        kern,
        grid=(n // tm,),
        in_specs=[pl.BlockSpec((tm, d), lambda i: (i, 0)),
                  pl.BlockSpec((d, LANES), lambda i: (0, 0)),
                  pl.BlockSpec((1, LANES), lambda i: (0, 0))],
        out_specs=[tile, tile],
        out_shape=[jax.ShapeDtypeStruct((n, LANES), F32)] * 2,
        scratch_shapes=[pltpu.VMEM((SUBLANES, LANES), F32)],
        compiler_params=_params("arbitrary"),
        name="logf",
    )(a, w_f, b_f)


def _online_softmax_update(chunks, cq, m_old, l_old):
    mx = functools.reduce(jnp.maximum, chunks)
    rowmax = jnp.broadcast_to(jnp.max(mx, axis=-1, keepdims=True), mx.shape)
    m_new = jnp.maximum(m_old, rowmax + cq)
    alpha = jnp.exp2(m_old - m_new)
    shift = m_new - cq
    p = [jnp.exp2(c - shift) for c in chunks]
    rowsum = jnp.broadcast_to(jnp.sum(functools.reduce(jnp.add, p), axis=-1, keepdims=True), mx.shape)
    return m_new, alpha * l_old + rowsum, alpha, p


def _fox_prompt_kernel(q_ref, k_ref, v_ref, cq_ref, ck_ref, *refs, tq, tk, n_cast):
    cast_in = refs[:n_cast]
    o_ref = refs[n_cast]
    cast_out = refs[n_cast + 1:2 * n_cast + 1]
    m_s, l_s, acc_s, cqb_s = refs[2 * n_cast + 1:]
    qi = pl.program_id(1)
    ki = pl.program_id(2)
    n_chunks = tk // LANES
    for src, dst in zip(cast_in, cast_out):
        dst[...] = src[...].astype(BF16)

    @pl.when(ki == 0)
    def _():
        m_s[...] = jnp.full(m_s.shape, -jnp.inf, F32)
        l_s[...] = jnp.zeros(l_s.shape, F32)
        acc_s[...] = jnp.zeros(acc_s.shape, F32)
        cq = cq_ref[0]
        for h in range(N_HEADS):
            cqb_s[h] = jnp.broadcast_to(cq[:, h:h + 1], cqb_s.shape[1:])

    def step(on_diagonal):
        ck = ck_ref[0]
        if on_diagonal:
            row = lax.broadcasted_iota(jnp.int32, (tq, LANES), 0)
            lane = lax.broadcasted_iota(jnp.int32, (tq, LANES), 1)
        for h in range(N_HEADS):
            cols = slice(h * HEAD_DIM, (h + 1) * HEAD_DIM)
            s = lax.dot_general(q_ref[0, :, cols], k_ref[0, :, cols],
                                (((1,), (1,)), ((), ())), preferred_element_type=F32)
            chunks = []
            for c in range(n_chunks):
                lanes = slice(c * LANES, (c + 1) * LANES)
                sc = s[:, lanes] - ck[h:h + 1, lanes]
                if on_diagonal:
                    sc = jnp.where(lane + c * LANES <= row, sc, NEG_BIG)
                chunks.append(sc)
            m_new, l_new, alpha, p = _online_softmax_update(chunks, cqb_s[h], m_s[h], l_s[h])
            m_s[h] = m_new
            l_s[h] = l_new
            pv = jnp.dot(jnp.concatenate(p, axis=1).astype(BF16), v_ref[0, :, cols],
                         preferred_element_type=F32)
            acc_s[:, cols] = alpha * acc_s[:, cols] + pv

    pl.when(ki < qi)(functools.partial(step, False))
    pl.when(ki == qi)(functools.partial(step, True))

    @pl.when(ki == pl.num_programs(2) - 1)
    def _():
        for h in range(N_HEADS):
            cols = slice(h * HEAD_DIM, (h + 1) * HEAD_DIM)
            o_ref[0, :, cols] = (acc_s[:, cols] / l_s[h]).astype(o_ref.dtype)


def _fox_prompt(q, k, v, c_rows, c_heads, *, tq, tk, cast=(), cast_layer=0):
    b, t, d = q.shape
    assert tq == tk
    nq, nk = t // tq, t // tk
    c_in, c_out, c_shape = _cast_specs(cast, cast_layer, b * nq * nk,
                                       lambda bi, qi, ki: (bi * nq + qi) * nk + ki)
    kv_map = lambda bi, qi, ki: (bi, jnp.minimum(ki, qi), 0)
    return pl.pallas_call(
        functools.partial(_fox_prompt_kernel, tq=tq, tk=tk, n_cast=len(cast)),
        grid=(b, nq, nk),
        in_specs=[pl.BlockSpec((1, tq, d), lambda bi, qi, ki: (bi, qi, 0)),
                  pl.BlockSpec((1, tk, d), kv_map),
                  pl.BlockSpec((1, tk, d), kv_map),
                  pl.BlockSpec((1, tq, LANES), lambda bi, qi, ki: (bi, qi, 0)),
                  pl.BlockSpec((1, N_HEADS, tk), lambda bi, qi, ki: (bi, 0, jnp.minimum(ki, qi))),
                  *c_in],
        out_specs=[pl.BlockSpec((1, tq, d), lambda bi, qi, ki: (bi, qi, 0)), *c_out],
        out_shape=[jax.ShapeDtypeStruct((b, t, d), BF16), *c_shape],
        scratch_shapes=[pltpu.VMEM((N_HEADS, tq, LANES), F32),
                        pltpu.VMEM((N_HEADS, tq, LANES), F32),
                        pltpu.VMEM((tq, d), F32),
                        pltpu.VMEM((N_HEADS, tq, LANES), F32)],
        compiler_params=_params("arbitrary", "arbitrary", "arbitrary"),
        name="fox_prompt",
    )(q, k, v, c_rows, c_heads, *cast)


def _page_suffix_kernel(lf_ref, out_ref, *, heads):
    x = lf_ref[...]
    width = x.shape[1]
    lane = lax.broadcasted_iota(jnp.int32, x.shape, 1)
    incl = x
    k = heads
    while k < width:
        incl = incl + jnp.where(lane < width - k, pltpu.roll(incl, width - k, axis=1), 0.0)
        k *= 2
    total = jnp.where(lane < heads, incl, 0.0)
    k = heads
    while k < width:
        total = total + pltpu.roll(total, k, axis=1)
        k *= 2
    out_ref[:, 0, :] = (incl - x) * LOG2E
    out_ref[:, 1, :] = total * LOG2E


def _page_suffix(logf_flat, *, heads, pages_per_step):
    n_pool, width = logf_flat.shape
    assert n_pool % pages_per_step == 0
    return pl.pallas_call(
        functools.partial(_page_suffix_kernel, heads=heads),
        grid=(n_pool // pages_per_step,),
        in_specs=[pl.BlockSpec((pages_per_step, width), lambda i: (i, 0))],
        out_specs=pl.BlockSpec((pages_per_step, 2, width), lambda i: (i, 0, 0)),
        out_shape=jax.ShapeDtypeStruct((n_pool, 2, width), F32),
        compiler_params=_params("arbitrary"),
        name="page_suffix",
    )(logf_flat)


def _fox_sample_step(step, n_steps, q_ref, kn_ref, vn_ref, cq_ref, cnrow_ref, k_refs, v_refs, s_refs,
                     o_ref, m_s, l_s, acc_s, carry_s, *, joint, t_new):
    group = len(k_refs)
    rows = q_ref.shape[1]
    heads = rows // t_new
    width = carry_s.shape[1]
    nt = (((1,), (1,)), ((), ()))
    q = q_ref[0]
    cq = cq_ref[0]

    def own_head(n_lanes):
        lane_head = lax.broadcasted_iota(jnp.int32, (t_new, n_lanes), 1) % heads
        return [lane_head == h for h in range(heads)]

    def compact(s, masks):
        out = s[(heads - 1) * t_new:, :]
        for h in range(heads - 2, -1, -1):
            out = jnp.where(masks[h], s[h * t_new:(h + 1) * t_new, :], out)
        return out

    def expand(p, masks):
        return jnp.concatenate([jnp.where(masks[h], p, 0.0) for h in range(heads)], axis=0)

    def per_head_rows(x):
        return jnp.concatenate([jnp.broadcast_to(x[:, h:h + 1], (t_new, LANES)) for h in range(heads)],
                               axis=0)

    def over_positions(x, op):
        k = heads
        while k < LANES:
            x = op(x, pltpu.roll(x, k, axis=1))
            k *= 2
        return x

    def update(blocks):
        tiles = [c[:, i * LANES:(i + 1) * LANES] for c, _, _ in blocks for i in range(c.shape[1] // LANES)]
        m_old = m_s[...]
        m_new = jnp.maximum(m_old, over_positions(functools.reduce(jnp.maximum, tiles), jnp.maximum) + cq)
        alpha = jnp.exp2(m_old - m_new)
        shift = m_new - cq
        acc = per_head_rows(alpha) * acc_s[...]
        l_add = jnp.zeros_like(m_old)
        start = 0
        for c, v, masks in blocks:
            n = c.shape[1] // LANES
            p_tiles = [jnp.exp2(t - shift) for t in tiles[start:start + n]]
            start += n
            l_add = l_add + functools.reduce(jnp.add, p_tiles)
            acc = acc + jnp.dot(expand(jnp.concatenate(p_tiles, axis=1), masks), v,
                                preferred_element_type=F32)
        m_s[...] = m_new
        l_s[...] = alpha * l_s[...] + over_positions(l_add, jnp.add)
        acc_s[...] = acc

    @pl.when(step == 0)
    def _():
        m_s[...] = jnp.full(m_s.shape, -jnp.inf, F32)
        l_s[...] = jnp.zeros(l_s.shape, F32)
        acc_s[...] = jnp.zeros(acc_s.shape, F32)
        carry_s[...] = jnp.zeros(carry_s.shape, F32)
        masks = own_head(LANES)
        s = compact(lax.dot_general(q, kn_ref[0], nt, preferred_element_type=F32), masks) - cnrow_ref[0]
        t_q = lax.broadcasted_iota(jnp.int32, s.shape, 0)
        t_k = lax.broadcasted_iota(jnp.int32, s.shape, 1) // heads
        update([(jnp.where(t_k <= t_q, s, NEG_BIG), vn_ref[0], masks)])

    masks = own_head(width)
    carry = carry_s[...]
    for g0 in range(0, group, joint):
        blocks = []
        for g in range(g0, g0 + joint):
            sfx = s_refs[g][0]
            kf = k_refs[g][0].reshape(width, HEAD_DIM)
            vf = v_refs[g][0].reshape(width, HEAD_DIM)
            s = compact(lax.dot_general(q, kf, nt, preferred_element_type=F32), masks)
            blocks.append((s + (carry + sfx[0:1, :]), vf, masks))
            carry = carry + sfx[1:2, :]
        update(blocks)
    carry_s[...] = carry

    @pl.when(step == n_steps - 1)
    def _():
        o_ref[0] = (acc_s[...] / per_head_rows(l_s[...])).astype(o_ref.dtype)


def _fox_sample_kernel(pt_ref, q_ref, kn_ref, vn_ref, cq_ref, cnrow_ref, *refs, group, joint, t_new):
    k_refs = refs[:group]
    v_refs = refs[group:2 * group]
    s_refs = refs[2 * group:3 * group]
    o_ref = refs[3 * group]
    _fox_sample_step(pl.program_id(1), pl.num_programs(1), q_ref, kn_ref, vn_ref, cq_ref, cnrow_ref,
                     k_refs, v_refs, s_refs, o_ref, *refs[3 * group + 1:], joint=joint, t_new=t_new)


def _fox_sample(page_table, q, k_new, v_new, cq_tile, cn_row, cache_k, cache_v, page_sfx,
                *, group, joint, t_new):
    b, rows, dh = q.shape
    n_pages = page_table.shape[1]
    _, page, heads, _ = cache_k.shape
    width = page * heads
    assert n_pages % group == 0 and group % joint == 0 and rows == heads * t_new and rows == LANES
    assert LANES % heads == 0

    def page_map(g, ndim):
        return lambda bi, si, pt: (pt[bi, n_pages - 1 - (si * group + g)],) + (0,) * (ndim - 1)

    per_seq = lambda shape: pl.BlockSpec((1,) + shape, lambda bi, si, pt: (bi, 0, 0))
    kv_specs = [pl.BlockSpec((1, page, heads, dh), page_map(g, 4)) for g in range(group)]
    sfx_specs = [pl.BlockSpec((1, 2, width), page_map(g, 3)) for g in range(group)]
    stat = pltpu.VMEM((t_new, LANES), F32)
    grid_spec = pltpu.PrefetchScalarGridSpec(
        num_scalar_prefetch=1,
        grid=(b, n_pages // group),
        in_specs=[per_seq((rows, dh)), per_seq((rows, dh)), per_seq((rows, dh)),
                  per_seq((t_new, LANES)), per_seq((1, rows)),
                  *kv_specs, *kv_specs, *sfx_specs],
        out_specs=per_seq((rows, dh)),
        scratch_shapes=[stat, stat, pltpu.VMEM((rows, dh), F32), pltpu.VMEM((1, width), F32)],
    )
    return pl.pallas_call(
        functools.partial(_fox_sample_kernel, group=group, joint=joint, t_new=t_new),
        grid_spec=grid_spec,
        out_shape=jax.ShapeDtypeStruct((b, rows, dh), BF16),
        compiler_params=_params("arbitrary", "arbitrary"),
        name="fox_sample",
    )(page_table, q, k_new, v_new, cq_tile, cn_row,
      *([cache_k] * group), *([cache_v] * group), *([page_sfx] * group))


def _side_cast(w, names, stacks, layer, n_steps):
    if names[0] in w:
        return ()
    if _can_cast_in(stacks, n_steps):
        return stacks
    w.update({name: s[layer].astype(BF16) for name, s in zip(names, stacks)})
    return ()


def _layer0_and_kv(x, seq_len, hist, kv16, q_dtype, w, *, tm):
    n, d = x.shape
    tn = 512
    names = ("w_up0", "w_down0")
    cast = _side_cast(w, names, w["mlp_f32"], 0, (n // tm) * (d // tn))
    gated, u_rows, *done = _inproj_conv(x, w["g_mix_pre"][0:1], w["w_in"], w["w_conv"], tm=tm, tn=tn,
                                        seq_len=seq_len, hist=hist, cast=cast, cast_layer=0)
    w.update(zip(names, done))
    row_tm = min(n, 512)
    h, a = _outproj_residual(gated, w["w_out"], x, w["g_mix_post"][0:1], w["g_ffn_pre"][0:1], tm=row_tm)
    g_next = jnp.concatenate([w["g_kv"], w["g_mix_pre"][1:2]], axis=0)
    names = ("w_kvf", "w_q", "w_o")
    cast = _side_cast(w, names, w["late_f32"], 0, n // row_tm)
    h, a_kv, a_q, *done = _mlp_residual(a, w["w_up0"], w["w_down0"], h, w["g_ffn_post"][0:1], g_next,
                                        tm=row_tm, tf=512, cast=cast)
    w.update(zip(names, done))
    ptm = min(n, 1024)
    ptn = 1024
    kv_dtypes = (F32, BF16) if kv16 else (F32,)
    kv = []
    for col0, name, stack in ((0, "w_up1", w["mlp_f32"][0]), (d, "w_down1", w["mlp_f32"][1])):
        cast = _side_cast(w, (name,), (stack,), 1, (n // ptm) * (d // ptn))
        *outs, = _proj(a_kv, w["w_kvf"], col0=col0, n_out=d, out_dtypes=kv_dtypes, tm=ptm, tn=ptn,
                       split_heads=(0,), cast=cast, cast_layer=1)
        w.update(zip((name,), outs[len(kv_dtypes):]))
        kv.append(outs[:len(kv_dtypes)])
    lf, c = _logf(a_kv, w["w_f"], w["b_f"], tm=min(n, 256), seq_len=seq_len)
    (q,) = _proj(a_q, w["w_q"], col0=0, n_out=d, out_dtypes=(q_dtype,), tm=ptm, tn=ptn,
                 scale=LOG2E * HEAD_DIM ** -0.5)
    return h, u_rows, kv[0], kv[1], lf, q, c * LOG2E


def kernel(x_prompt, x_sample, state_conv, cache_k, cache_v, cache_logf, page_table,
           w_in_a, w_conv_a, w_out_a, g_kv, w_kvf, b_f, w_q, w_o,
           g_mix_pre, g_mix_post, g_ffn_pre, g_ffn_post, w_up, w_down):
    bp, t, d = x_prompt.shape
    bs, ts, _ = x_sample.shape
    heads = N_HEADS
    w = {
        "w_in": w_in_a[0].astype(BF16), "w_conv": w_conv_a[0], "w_out": w_out_a[0].astype(BF16),
        "g_kv": g_kv[None, :],
        "w_f": jnp.pad(w_kvf[:, 2 * d:], ((0, 0), (0, LANES - heads))).astype(BF16),
        "b_f": jnp.pad(b_f, (0, LANES - heads))[None, :],
        "g_mix_pre": g_mix_pre, "g_mix_post": g_mix_post,
        "g_ffn_pre": g_ffn_pre, "g_ffn_post": g_ffn_post,
        "mlp_f32": (w_up, w_down), "late_f32": (w_kvf[None], w_q, w_o),
    }

    n_p = bp * t
    tm_p = min(t, 1024)
    h_p, u_p, (k_p, k16), (v_p, v16), lf_p, q_p, c_p = _layer0_and_kv(
        x_prompt.reshape(n_p, d), t, None, True, BF16, w, tm=tm_p)
    conv_prompt = u_p.reshape(bp, t // tm_p, SUBLANES, d)[:, -1, SUBLANES - (CONV_W - 1):, :][None]
    c3 = c_p.reshape(bp, t, LANES)
    (o_p,) = _fox_prompt(q_p.reshape(bp, t, d), k16.reshape(bp, t, d), v16.reshape(bp, t, d),
                         c3, jnp.swapaxes(c3[:, :, :heads], 1, 2), tq=512, tk=512)
    h_p, a_p = _outproj_residual(o_p.reshape(n_p, d), w["w_o"], h_p, w["g_mix_post"][1:2],
                                 w["g_ffn_pre"][1:2], tm=512)
    (y_p,) = _mlp_residual(a_p, w["w_up1"], w["w_down1"], h_p, w["g_ffn_post"][1:2], None,
                           tm=min(n_p, 1024), tf=512)

    n_s = bs * ts
    prev = state_conv[0]
    zeros = jnp.zeros((bs, ts - 2, d), F32)
    p1 = jnp.concatenate([prev[:, 1:2], zeros, zeros[:, :1]], axis=1).reshape(n_s, d)
    p2 = jnp.concatenate([prev, zeros], axis=1).reshape(n_s, d)
    h_s, u_s, (k_s,), (v_s,), lf_s, q_s, c_s = _layer0_and_kv(
        x_sample.reshape(n_s, d), ts, (p1, p2), False, F32, w, tm=n_s)
    conv_sample = u_s.reshape(bs, ts, d)[:, ts - (CONV_W - 1):, :][None]
    n_pool, page = cache_logf.shape[:2]
    page_sfx = _page_suffix(cache_logf.reshape(n_pool, page * heads), heads=heads,
                            pages_per_step=64 if n_pool % 64 == 0 else n_pool)
    q_rows = q_s.reshape(bs, ts, heads, HEAD_DIM).transpose(0, 2, 1, 3).reshape(bs, heads * ts, HEAD_DIM)
    cn = c_s[:, :heads].reshape(bs, ts, heads)
    o_s = _fox_sample(page_table, q_rows,
                      k_s.reshape(bs, ts * heads, HEAD_DIM), v_s.reshape(bs, ts * heads, HEAD_DIM),
                      jnp.tile(cn, (1, 1, LANES // heads)), cn.reshape(bs, 1, ts * heads),
                      cache_k, cache_v, page_sfx, group=8, joint=SAMPLE_JOINT, t_new=ts)
    o_s = o_s.reshape(bs, heads, ts, HEAD_DIM).transpose(0, 2, 1, 3).reshape(n_s, d)
    h_s, a_s = _outproj_residual(o_s, w["w_o"], h_s, w["g_mix_post"][1:2], w["g_ffn_pre"][1:2], tm=n_s)
    (y_s,) = _mlp_residual(a_s, w["w_up1"], w["w_down1"], h_s, w["g_ffn_post"][1:2], None,
                           tm=n_s, tf=512)

    return (y_p.reshape(bp, t, d), y_s.reshape(bs, ts, d), conv_prompt, conv_sample,
            k_p.reshape(bp, t, heads, HEAD_DIM), v_p.reshape(bp, t, heads, HEAD_DIM),
            lf_p[:, :heads].reshape(bp, t, heads),
            k_s.reshape(bs, ts, heads, HEAD_DIM), v_s.reshape(bs, ts, heads, HEAD_DIM),
            lf_s[:, :heads].reshape(bs, ts, heads))
```

```python
import functools
import math

import jax
import jax.numpy as jnp
from jax import lax
from jax.experimental import pallas as pl
from jax.experimental.pallas import tpu as pltpu

RMS_EPS = 1e-6
CONV_W = 3
N_HEADS = 16
HEAD_DIM = 128
LANES = 128
SUBLANES = 8
VMEM_LIMIT = 60 * 1024 * 1024
NEG_BIG = -1e30
LOG2E = math.log2(math.e)
SAMPLE_JOINT = 4

F32 = jnp.float32
BF16 = jnp.bfloat16


def _params(*sem):
    return pltpu.CompilerParams(dimension_semantics=sem, vmem_limit_bytes=VMEM_LIMIT)


def _rms(x, g):
    return x * lax.rsqrt(jnp.mean(x * x, axis=-1, keepdims=True) + RMS_EPS) * g


def _split3(x):
    a = x.astype(BF16)
    r = x - a.astype(F32)
    b = r.astype(BF16)
    c = (r - b.astype(F32)).astype(BF16)
    return a, b, c


def _dot01_exact(m01, x):
    a, b, c = _split3(x)
    d = functools.partial(jnp.dot, preferred_element_type=F32)
    return d(m01, a) + d(m01, b) + d(m01, c)


def _cast_specs(stacks, layer, n_steps, step_of):
    in_specs, out_specs, out_shape = [], [], []
    for stack in stacks:
        r, c = stack.shape[-2:]
        rows = r // n_steps
        assert rows * n_steps == r and rows % (2 * SUBLANES) == 0
        if stack.ndim == 3:
            in_specs.append(pl.BlockSpec((None, rows, c), lambda *g: (layer, step_of(*g), 0)))
        else:
            in_specs.append(pl.BlockSpec((rows, c), lambda *g: (step_of(*g), 0)))
        out_specs.append(pl.BlockSpec((rows, c), lambda *g: (step_of(*g), 0)))
        out_shape.append(jax.ShapeDtypeStruct((r, c), BF16))
    return in_specs, out_specs, out_shape


def _can_cast_in(stacks, n_steps):
    return all(s.shape[-2] % (n_steps * 2 * SUBLANES) == 0 for s in stacks)


def _inproj_conv_kernel(*refs, tm, tiles_per_seq, seq_len, inject, n_cast):
    n_in = 8 if inject else 6
    cast_in = refs[n_in:n_in + n_cast]
    cast_out = refs[n_in + n_cast + 2:n_in + 2 * n_cast + 2]
    core = refs[:n_in] + refs[n_in + n_cast:n_in + n_cast + 2] + refs[n_in + 2 * n_cast + 2:]
    if inject:
        (x_ref, g_ref, wb_ref, wc_ref, wx_ref, wconv_ref, p1_ref, p2_ref,
         gated_ref, u_ref, a_s) = core
    else:
        (x_ref, g_ref, wb_ref, wc_ref, wx_ref, wconv_ref,
         gated_ref, u_ref, a_s, carry_s) = core
    i = pl.program_id(0)
    j = pl.program_id(1)
    for src, dst in zip(cast_in, cast_out):
        dst[...] = src[...].astype(BF16)

    @pl.when(j == 0)
    def _():
        a_s[...] = _rms(x_ref[...], g_ref[...]).astype(BF16)

    a = a_s[...]
    dot = functools.partial(jnp.dot, preferred_element_type=F32)
    u = dot(a, wc_ref[...]) * dot(a, wx_ref[...])
    rolled1 = pltpu.roll(u, 1, axis=0)
    rolled2 = pltpu.roll(u, 2, axis=0)
    row = lax.broadcasted_iota(jnp.int32, u.shape, 0)
    if inject:
        pos = row % seq_len
        u1 = jnp.where(pos >= 1, rolled1, p1_ref[...])
        u2 = jnp.where(pos >= 2, rolled2, p2_ref[...])
        u_ref[...] = u
    else:
        @pl.when(i % tiles_per_seq == 0)
        def _():
            carry_s[j] = jnp.zeros(carry_s.shape[1:], F32)

        tail = carry_s[j]
        t1 = tail[SUBLANES - 1:SUBLANES, :]
        t2 = tail[SUBLANES - 2:SUBLANES - 1, :]
        u1 = jnp.where(row == 0, t1, rolled1)
        u2 = jnp.where(row == 0, t2, jnp.where(row == 1, t1, rolled2))
        last = u[tm - SUBLANES:, :]
        carry_s[j] = last
        u_ref[...] = last
    wconv = wconv_ref[...]
    conv = wconv[0:1, :] * u2 + wconv[1:2, :] * u1 + wconv[2:3, :] * u
    gated_ref[...] = (dot(a, wb_ref[...]) * conv).astype(BF16)


def _inproj_conv(x, g, w_in, w_conv, *, tm, tn, seq_len, hist=None, cast=(), cast_layer=0):
    n, d = x.shape
    nj = d // tn
    c_in, c_out, c_shape = _cast_specs(cast, cast_layer, (n // tm) * nj, lambda i, j: i * nj + j)
    inject = hist is not None
    assert n % tm == 0 and d % tn == 0
    if not inject:
        assert seq_len % tm == 0
    x_spec = pl.BlockSpec((tm, d), lambda i, j: (i, 0))
    g_spec = pl.BlockSpec((1, d), lambda i, j: (0, 0))
    w_specs = [pl.BlockSpec((d, tn), lambda i, j, s=s: (0, j + s * nj)) for s in range(3)]
    wconv_spec = pl.BlockSpec((CONV_W, tn), lambda i, j: (0, j))
    tile_spec = pl.BlockSpec((tm, tn), lambda i, j: (i, j))
    in_specs = [x_spec, g_spec, *w_specs, wconv_spec]
    args = [x, g, w_in, w_in, w_in, w_conv]
    scratch = [pltpu.VMEM((tm, d), BF16)]
    if inject:
        in_specs += [tile_spec, tile_spec]
        args += list(hist)
        u_shape = jax.ShapeDtypeStruct((n, d), F32)
        u_spec = tile_spec
    else:
        scratch.append(pltpu.VMEM((nj, SUBLANES, tn), F32))
        u_shape = jax.ShapeDtypeStruct((n // tm * SUBLANES, d), F32)
        u_spec = pl.BlockSpec((SUBLANES, tn), lambda i, j: (i, j))
    kern = functools.partial(_inproj_conv_kernel, tm=tm, tiles_per_seq=max(seq_len // tm, 1),
                             seq_len=seq_len, inject=inject, n_cast=len(cast))
    return pl.pallas_call(
        kern,
        grid=(n // tm, nj),
        in_specs=in_specs + c_in,
        out_specs=[tile_spec, u_spec] + c_out,
        out_shape=[jax.ShapeDtypeStruct((n, d), BF16), u_shape] + c_shape,
        scratch_shapes=scratch,
        compiler_params=_params("arbitrary", "arbitrary"),
        name="inproj_conv",
    )(*args, *cast)


def _outproj_kernel(a_ref, w_ref, h_ref, gpost_ref, gnext_ref, h1_ref, anext_ref):
    mix = jnp.dot(a_ref[...], w_ref[...], preferred_element_type=F32)
    h1 = h_ref[...] + _rms(mix, gpost_ref[...])
    h1_ref[...] = h1
    anext_ref[...] = _rms(h1, gnext_ref[...]).astype(BF16)


def _outproj_residual(a, w, h, g_post, g_next, *, tm):
    n, d = h.shape
    row = pl.BlockSpec((tm, d), lambda i: (i, 0))
    vec = pl.BlockSpec((1, d), lambda i: (0, 0))
    return pl.pallas_call(
        _outproj_kernel,
        grid=(n // tm,),
        in_specs=[row, pl.BlockSpec((d, d), lambda i: (0, 0)), row, vec, vec],
        out_specs=[row, row],
        out_shape=[jax.ShapeDtypeStruct((n, d), F32), jax.ShapeDtypeStruct((n, d), BF16)],
        compiler_params=_params("arbitrary"),
        name="outproj_residual",
    )(a, w, h, g_post, g_next)


def _mlp_kernel(*refs, n_next, own_acc, n_cast):
    a_ref, wup_ref, wdown_ref, h_ref, gpost_ref, gnext_ref = refs[:6]
    cast_in = refs[6:6 + n_cast]
    h2_ref = refs[6 + n_cast]
    next_refs = refs[7 + n_cast:7 + n_cast + n_next]
    cast_out = refs[7 + n_cast + n_next:7 + 2 * n_cast + n_next]
    acc = refs[7 + 2 * n_cast + n_next] if own_acc else h2_ref
    f = pl.program_id(1)

    @pl.when(f == 0)
    def _():
        acc[...] = jnp.zeros(acc.shape, F32)
        for src, dst in zip(cast_in, cast_out):
            dst[...] = src[...].astype(BF16)

    hid = jnp.dot(a_ref[...], wup_ref[...], preferred_element_type=F32)
    hid = jnp.square(jnp.maximum(hid, 0.0)).astype(BF16)
    acc[...] += jnp.dot(hid, wdown_ref[...], preferred_element_type=F32)

    @pl.when(f == pl.num_programs(1) - 1)
    def _():
        h2 = h_ref[...] + _rms(acc[...], gpost_ref[...])
        h2_ref[...] = h2
        for k, ref in enumerate(next_refs):
            ref[...] = _rms(h2, gnext_ref[k:k + 1, :]).astype(BF16)


def _mlp_residual(a, w_up, w_down, h, g_post, g_next, *, tm, tf, cast=()):
    n, d = h.shape
    d_ff = w_up.shape[1]
    n_next = 0 if g_next is None else g_next.shape[0]
    g_arr = jnp.ones((1, d), F32) if g_next is None else g_next
    own_acc = n_next > 0
    c_in, c_out, c_shape = _cast_specs(cast, 0, n // tm, lambda i, f: i)
    row = pl.BlockSpec((tm, d), lambda i, f: (i, 0))
    outs = [jax.ShapeDtypeStruct((n, d), F32)] + [jax.ShapeDtypeStruct((n, d), BF16)] * n_next
    return pl.pallas_call(
        functools.partial(_mlp_kernel, n_next=n_next, own_acc=own_acc, n_cast=len(cast)),
        grid=(n // tm, d_ff // tf),
        in_specs=[row,
                  pl.BlockSpec((d, tf), lambda i, f: (0, f)),
                  pl.BlockSpec((tf, d), lambda i, f: (f, 0)),
                  row,
                  pl.BlockSpec((1, d), lambda i, f: (0, 0)),
                  pl.BlockSpec(g_arr.shape, lambda i, f: (0, 0)),
                  *c_in],
        out_specs=[row] * (1 + n_next) + c_out,
        out_shape=outs + c_shape,
        scratch_shapes=[pltpu.VMEM((tm, d), F32)] if own_acc else [],
        compiler_params=_params("arbitrary", "arbitrary"),
        name="mlp_residual",
    )(a, w_up, w_down, h, g_post, g_arr, *cast)


def _proj_kernel(a_ref, w_ref, *out_refs, scale):
    y = jnp.dot(a_ref[...], w_ref[...], preferred_element_type=F32)
    if scale is not None:
        y = y * scale
    for ref in out_refs:
        if len(ref.shape) == 3:
            per_head = [y[:, hh * HEAD_DIM:(hh + 1) * HEAD_DIM] for hh in range(ref.shape[1])]
            ref[...] = jnp.swapaxes(jnp.stack(per_head, axis=0), 0, 1).astype(ref.dtype)
        else:
            ref[...] = y.astype(ref.dtype)


def _proj(a, w, *, col0, n_out, out_dtypes, tm, tn, scale=None, split_heads=()):
    n, d = a.shape
    assert col0 % tn == 0 and n_out % tn == 0 and tn % HEAD_DIM == 0
    jb = col0 // tn
    tile = pl.BlockSpec((tm, tn), lambda i, j: (i, j))
    head_tile = pl.BlockSpec((tm, tn // HEAD_DIM, HEAD_DIM), lambda i, j: (i, j, 0))
    out_specs, out_shape = [], []
    for k, dt in enumerate(out_dtypes):
        if k in split_heads:
            out_specs.append(head_tile)
            out_shape.append(jax.ShapeDtypeStruct((n, n_out // HEAD_DIM, HEAD_DIM), dt))
        else:
            out_specs.append(tile)
            out_shape.append(jax.ShapeDtypeStruct((n, n_out), dt))
    return pl.pallas_call(
        functools.partial(_proj_kernel, scale=scale),
        grid=(n // tm, n_out // tn),
        in_specs=[pl.BlockSpec((tm, d), lambda i, j: (i, 0)),
                  pl.BlockSpec((d, tn), lambda i, j: (0, j + jb))],
        out_specs=out_specs,
        out_shape=out_shape,
        compiler_params=_params("arbitrary", "arbitrary"),
        name="proj",
    )(a, w)


def _logf_kernel(a_ref, wf_ref, bf_ref, lf_ref, c_ref, carry_s, *, tm, tiles_per_seq, seq_len):
    i = pl.program_id(0)
    x = jnp.dot(a_ref[...], wf_ref[...], preferred_element_type=F32) + bf_ref[...]
    lf = jnp.minimum(x, 0.0) - jnp.log1p(jnp.exp(-jnp.abs(x)))
    lf_ref[...] = lf
    r = lax.broadcasted_iota(jnp.int32, (tm, tm), 0)
    c = lax.broadcasted_iota(jnp.int32, (tm, tm), 1)
    same_seq = (r // seq_len) == (c // seq_len) if seq_len < tm else (r >= 0)
    tri = jnp.where((c <= r) & same_seq, 1.0, 0.0).astype(BF16)
    csum = _dot01_exact(tri, lf)

    @pl.when(i % tiles_per_seq == 0)
    def _():
        carry_s[...] = jnp.zeros(carry_s.shape, F32)

    csum = csum + carry_s[0:1, :]
    c_ref[...] = csum
    carry_s[...] = jnp.broadcast_to(csum[tm - 1:tm, :], carry_s.shape)


def _logf(a, w_f, b_f, *, tm, seq_len):
    n, d = a.shape
    assert seq_len % tm == 0 or tm % seq_len == 0
    tile = pl.BlockSpec((tm, LANES), lambda i: (i, 0))
    kern = functools.partial(_logf_kernel, tm=tm, tiles_per_seq=max(seq_len // tm, 1), seq_len=seq_len)
    return pl.pallas_call(
        kern,
        grid=(n // tm,),
        in_specs=[pl.BlockSpec((tm, d), lambda i: (i, 0)),
                  pl.BlockSpec((d, LANES), lambda i: (0, 0)),
                  pl.BlockSpec((1, LANES), lambda i: (0, 0))],
        out_specs=[tile, tile],
        out_shape=[jax.ShapeDtypeStruct((n, LANES), F32)] * 2,
        scratch_shapes=[pltpu.VMEM((SUBLANES, LANES), F32)],
        compiler_params=_params("arbitrary"),
        name="logf",
    )(a, w_f, b_f)


def _online_softmax_update(chunks, cq, m_old, l_old):
    mx = functools.reduce(jnp.maximum, chunks)
    rowmax = jnp.broadcast_to(jnp.max(mx, axis=-1, keepdims=True), mx.shape)
    m_new = jnp.maximum(m_old, rowmax + cq)
    alpha = jnp.exp2(m_old - m_new)
    shift = m_new - cq
    p = [jnp.exp2(c - shift) for c in chunks]
    rowsum = jnp.broadcast_to(jnp.sum(functools.reduce(jnp.add, p), axis=-1, keepdims=True), mx.shape)
    return m_new, alpha * l_old + rowsum, alpha, p


def _fox_prompt_kernel(q_ref, k_ref, v_ref, cq_ref, ck_ref, *refs, tq, tk, n_cast):
    cast_in = refs[:n_cast]
    o_ref = refs[n_cast]
    cast_out = refs[n_cast + 1:2 * n_cast + 1]
    m_s, l_s, acc_s, cqb_s = refs[2 * n_cast + 1:]
    qi = pl.program_id(1)
    ki = pl.program_id(2)
    n_chunks = tk // LANES
    for src, dst in zip(cast_in, cast_out):
        dst[...] = src[...].astype(BF16)

    @pl.when(ki == 0)
    def _():
        m_s[...] = jnp.full(m_s.shape, -jnp.inf, F32)
        l_s[...] = jnp.zeros(l_s.shape, F32)
        acc_s[...] = jnp.zeros(acc_s.shape, F32)
        cq = cq_ref[0]
        for h in range(N_HEADS):
            cqb_s[h] = jnp.broadcast_to(cq[:, h:h + 1], cqb_s.shape[1:])

    def step(on_diagonal):
        ck = ck_ref[0]
        if on_diagonal:
            row = lax.broadcasted_iota(jnp.int32, (tq, LANES), 0)
            lane = lax.broadcasted_iota(jnp.int32, (tq, LANES), 1)
        for h in range(N_HEADS):
            cols = slice(h * HEAD_DIM, (h + 1) * HEAD_DIM)
            s = lax.dot_general(q_ref[0, :, cols], k_ref[0, :, cols],
                                (((1,), (1,)), ((), ())), preferred_element_type=F32)
            chunks = []
            for c in range(n_chunks):
                lanes = slice(c * LANES, (c + 1) * LANES)
                sc = s[:, lanes] - ck[h:h + 1, lanes]
                if on_diagonal:
                    sc = jnp.where(lane + c * LANES <= row, sc, NEG_BIG)
                chunks.append(sc)
            m_new, l_new, alpha, p = _online_softmax_update(chunks, cqb_s[h], m_s[h], l_s[h])
            m_s[h] = m_new
            l_s[h] = l_new
            pv = jnp.dot(jnp.concatenate(p, axis=1).astype(BF16), v_ref[0, :, cols],
                         preferred_element_type=F32)
            acc_s[:, cols] = alpha * acc_s[:, cols] + pv

    pl.when(ki < qi)(functools.partial(step, False))
    pl.when(ki == qi)(functools.partial(step, True))

    @pl.when(ki == pl.num_programs(2) - 1)
    def _():
        for h in range(N_HEADS):
            cols = slice(h * HEAD_DIM, (h + 1) * HEAD_DIM)
            o_ref[0, :, cols] = (acc_s[:, cols] / l_s[h]).astype(o_ref.dtype)


def _fox_prompt(q, k, v, c_rows, c_heads, *, tq, tk, cast=(), cast_layer=0):
    b, t, d = q.shape
    assert tq == tk
    nq, nk = t // tq, t // tk
    c_in, c_out, c_shape = _cast_specs(cast, cast_layer, b * nq * nk,
                                       lambda bi, qi, ki: (bi * nq + qi) * nk + ki)
    kv_map = lambda bi, qi, ki: (bi, jnp.minimum(ki, qi), 0)
    return pl.pallas_call(
        functools.partial(_fox_prompt_kernel, tq=tq, tk=tk, n_cast=len(cast)),
        grid=(b, nq, nk),
        in_specs=[pl.BlockSpec((1, tq, d), lambda bi, qi, ki: (bi, qi, 0)),
                  pl.BlockSpec((1, tk, d), kv_map),
                  pl.BlockSpec((1, tk, d), kv_map),
                  pl.BlockSpec((1, tq, LANES), lambda bi, qi, ki: (bi, qi, 0)),
                  pl.BlockSpec((1, N_HEADS, tk), lambda bi, qi, ki: (bi, 0, jnp.minimum(ki, qi))),
                  *c_in],
        out_specs=[pl.BlockSpec((1, tq, d), lambda bi, qi, ki: (bi, qi, 0)), *c_out],
        out_shape=[jax.ShapeDtypeStruct((b, t, d), BF16), *c_shape],
        scratch_shapes=[pltpu.VMEM((N_HEADS, tq, LANES), F32),
                        pltpu.VMEM((N_HEADS, tq, LANES), F32),
                        pltpu.VMEM((tq, d), F32),
                        pltpu.VMEM((N_HEADS, tq, LANES), F32)],
        compiler_params=_params("arbitrary", "arbitrary", "arbitrary"),
        name="fox_prompt",
    )(q, k, v, c_rows, c_heads, *cast)


def _page_suffix_kernel(lf_ref, out_ref, *, heads):
    x = lf_ref[...]
    width = x.shape[1]
    lane = lax.broadcasted_iota(jnp.int32, x.shape, 1)
    incl = x
    k = heads
    while k < width:
        incl = incl + jnp.where(lane < width - k, pltpu.roll(incl, width - k, axis=1), 0.0)
        k *= 2
    total = jnp.where(lane < heads, incl, 0.0)
    k = heads
    while k < width:
        total = total + pltpu.roll(total, k, axis=1)
        k *= 2
    out_ref[:, 0, :] = (incl - x) * LOG2E
    out_ref[:, 1, :] = total * LOG2E


def _page_suffix(logf_flat, *, heads, pages_per_step):
    n_pool, width = logf_flat.shape
    assert n_pool % pages_per_step == 0
    return pl.pallas_call(
        functools.partial(_page_suffix_kernel, heads=heads),
        grid=(n_pool // pages_per_step,),
        in_specs=[pl.BlockSpec((pages_per_step, width), lambda i: (i, 0))],
        out_specs=pl.BlockSpec((pages_per_step, 2, width), lambda i: (i, 0, 0)),
        out_shape=jax.ShapeDtypeStruct((n_pool, 2, width), F32),
        compiler_params=_params("arbitrary"),
        name="page_suffix",
    )(logf_flat)


def _fox_sample_step(step, n_steps, q_ref, kn_ref, vn_ref, cq_ref, cnrow_ref, k_refs, v_refs, s_refs,
                     o_ref, m_s, l_s, acc_s, carry_s, *, joint, t_new):
    group = len(k_refs)
    rows = q_ref.shape[1]
    heads = rows // t_new
    width = carry_s.shape[1]
    nt = (((1,), (1,)), ((), ()))
    q = q_ref[0]
    cq = cq_ref[0]

    def own_head(n_lanes):
        lane_head = lax.broadcasted_iota(jnp.int32, (t_new, n_lanes), 1) % heads
        return [lane_head == h for h in range(heads)]

    def compact(s, masks):
        out = s[(heads - 1) * t_new:, :]
        for h in range(heads - 2, -1, -1):
            out = jnp.where(masks[h], s[h * t_new:(h + 1) * t_new, :], out)
        return out

    def expand(p, masks):
        return jnp.concatenate([jnp.where(masks[h], p, 0.0) for h in range(heads)], axis=0)

    def per_head_rows(x):
        return jnp.concatenate([jnp.broadcast_to(x[:, h:h + 1], (t_new, LANES)) for h in range(heads)],
                               axis=0)

    def over_positions(x, op):
        k = heads
        while k < LANES:
            x = op(x, pltpu.roll(x, k, axis=1))
            k *= 2
        return x

    def update(blocks):
        tiles = [c[:, i * LANES:(i + 1) * LANES] for c, _, _ in blocks for i in range(c.shape[1] // LANES)]
        m_old = m_s[...]
        m_new = jnp.maximum(m_old, over_positions(functools.reduce(jnp.maximum, tiles), jnp.maximum) + cq)
        alpha = jnp.exp2(m_old - m_new)
        shift = m_new - cq
        acc = per_head_rows(alpha) * acc_s[...]
        l_add = jnp.zeros_like(m_old)
        start = 0
        for c, v, masks in blocks:
            n = c.shape[1] // LANES
            p_tiles = [jnp.exp2(t - shift) for t in tiles[start:start + n]]
            start += n
            l_add = l_add + functools.reduce(jnp.add, p_tiles)
            acc = acc + jnp.dot(expand(jnp.concatenate(p_tiles, axis=1), masks), v,
                                preferred_element_type=F32)
        m_s[...] = m_new
        l_s[...] = alpha * l_s[...] + over_positions(l_add, jnp.add)
        acc_s[...] = acc

    @pl.when(step == 0)
    def _():
        m_s[...] = jnp.full(m_s.shape, -jnp.inf, F32)
        l_s[...] = jnp.zeros(l_s.shape, F32)
        acc_s[...] = jnp.zeros(acc_s.shape, F32)
        carry_s[...] = jnp.zeros(carry_s.shape, F32)
        masks = own_head(LANES)
        s = compact(lax.dot_general(q, kn_ref[0], nt, preferred_element_type=F32), masks) - cnrow_ref[0]
        t_q = lax.broadcasted_iota(jnp.int32, s.shape, 0)
        t_k = lax.broadcasted_iota(jnp.int32, s.shape, 1) // heads
        update([(jnp.where(t_k <= t_q, s, NEG_BIG), vn_ref[0], masks)])

    masks = own_head(width)
    carry = carry_s[...]
    for g0 in range(0, group, joint):
        blocks = []
        for g in range(g0, g0 + joint):
            sfx = s_refs[g][0]
            kf = k_refs[g][0].reshape(width, HEAD_DIM)
            vf = v_refs[g][0].reshape(width, HEAD_DIM)
            s = compact(lax.dot_general(q, kf, nt, preferred_element_type=F32), masks)
            blocks.append((s + (carry + sfx[0:1, :]), vf, masks))
            carry = carry + sfx[1:2, :]
        update(blocks)
    carry_s[...] = carry

    @pl.when(step == n_steps - 1)
    def _():
        o_ref[0] = (acc_s[...] / per_head_rows(l_s[...])).astype(o_ref.dtype)


def _fox_sample_kernel(pt_ref, q_ref, kn_ref, vn_ref, cq_ref, cnrow_ref, *refs, group, joint, t_new):
    k_refs = refs[:group]
    v_refs = refs[group:2 * group]
    s_refs = refs[2 * group:3 * group]
    o_ref = refs[3 * group]
    _fox_sample_step(pl.program_id(1), pl.num_programs(1), q_ref, kn_ref, vn_ref, cq_ref, cnrow_ref,
                     k_refs, v_refs, s_refs, o_ref, *refs[3 * group + 1:], joint=joint, t_new=t_new)


def _fox_sample(page_table, q, k_new, v_new, cq_tile, cn_row, cache_k, cache_v, page_sfx,
                *, group, joint, t_new):
    b, rows, dh = q.shape
    n_pages = page_table.shape[1]
    _, page, heads, _ = cache_k.shape
    width = page * heads
    assert n_pages % group == 0 and group % joint == 0 and rows == heads * t_new and rows == LANES
    assert LANES % heads == 0

    def page_map(g, ndim):
        return lambda bi, si, pt: (pt[bi, n_pages - 1 - (si * group + g)],) + (0,) * (ndim - 1)

    per_seq = lambda shape: pl.BlockSpec((1,) + shape, lambda bi, si, pt: (bi, 0, 0))
    kv_specs = [pl.BlockSpec((1, page, heads, dh), page_map(g, 4)) for g in range(group)]
    sfx_specs = [pl.BlockSpec((1, 2, width), page_map(g, 3)) for g in range(group)]
    stat = pltpu.VMEM((t_new, LANES), F32)
    grid_spec = pltpu.PrefetchScalarGridSpec(
        num_scalar_prefetch=1,
        grid=(b, n_pages // group),
        in_specs=[per_seq((rows, dh)), per_seq((rows, dh)), per_seq((rows, dh)),
                  per_seq((t_new, LANES)), per_seq((1, rows)),
                  *kv_specs, *kv_specs, *sfx_specs],
        out_specs=per_seq((rows, dh)),
        scratch_shapes=[stat, stat, pltpu.VMEM((rows, dh), F32), pltpu.VMEM((1, width), F32)],
    )
    return pl.pallas_call(
        functools.partial(_fox_sample_kernel, group=group, joint=joint, t_new=t_new),
        grid_spec=grid_spec,
        out_shape=jax.ShapeDtypeStruct((b, rows, dh), BF16),
        compiler_params=_params("arbitrary", "arbitrary"),
        name="fox_sample",
    )(page_table, q, k_new, v_new, cq_tile, cn_row,
      *([cache_k] * group), *([cache_v] * group), *([page_sfx] * group))


def _side_cast(w, names, stacks, layer, n_steps):
    if names[0] in w:
        return ()
    if _can_cast_in(stacks, n_steps):
        return stacks
    w.update({name: (s[layer] if s.ndim == 3 else s).astype(BF16) for name, s in zip(names, stacks)})
    return ()


def _layer0_and_kv(x, seq_len, hist, kv16, q_dtype, w, *, tm):
    n, d = x.shape
    tn = 512
    names = ("w_up0", "w_down0")
    cast = _side_cast(w, names, w["mlp_f32"], 0, (n // tm) * (d // tn))
    gated, u_rows, *done = _inproj_conv(x, w["g_mix_pre"][0:1], w["w_in"], w["w_conv"], tm=tm, tn=tn,
                                        seq_len=seq_len, hist=hist, cast=cast, cast_layer=0)
    w.update(zip(names, done))
    row_tm = min(n, 512)
    h, a = _outproj_residual(gated, w["w_out"], x, w["g_mix_post"][0:1], w["g_ffn_pre"][0:1], tm=row_tm)
    g_next = jnp.concatenate([w["g_kv"], w["g_mix_pre"][1:2]], axis=0)
    names = ("w_kvf", "w_q", "w_o")
    cast = _side_cast(w, names, w["late_f32"], 0, n // row_tm)
    h, a_kv, a_q, *done = _mlp_residual(a, w["w_up0"], w["w_down0"], h, w["g_ffn_post"][0:1], g_next,
                                        tm=row_tm, tf=512, cast=cast)
    w.update(zip(names, done))
    ptm = min(n, 1024)
    ptn = 1024
    kv_dtypes = (F32, BF16) if kv16 else (F32,)
    k = _proj(a_kv, w["w_kvf"], col0=0, n_out=d, out_dtypes=kv_dtypes, tm=ptm, tn=ptn, split_heads=(0,))
    v = _proj(a_kv, w["w_kvf"], col0=d, n_out=d, out_dtypes=kv_dtypes, tm=ptm, tn=ptn, split_heads=(0,))
    lf, c = _logf(a_kv, w["w_f"], w["b_f"], tm=min(n, 256), seq_len=seq_len)
    (q,) = _proj(a_q, w["w_q"], col0=0, n_out=d, out_dtypes=(q_dtype,), tm=ptm, tn=ptn,
                 scale=LOG2E * HEAD_DIM ** -0.5)
    return h, u_rows, k, v, lf, q, c * LOG2E


def kernel(x_prompt, x_sample, state_conv, cache_k, cache_v, cache_logf, page_table,
           w_in_a, w_conv_a, w_out_a, g_kv, w_kvf, b_f, w_q, w_o,
           g_mix_pre, g_mix_post, g_ffn_pre, g_ffn_post, w_up, w_down):
    bp, t, d = x_prompt.shape
    bs, ts, _ = x_sample.shape
    heads = N_HEADS
    w = {
        "w_in": w_in_a[0].astype(BF16), "w_conv": w_conv_a[0], "w_out": w_out_a[0].astype(BF16),
        "g_kv": g_kv[None, :],
        "w_f": jnp.pad(w_kvf[:, 2 * d:], ((0, 0), (0, LANES - heads))).astype(BF16),
        "b_f": jnp.pad(b_f, (0, LANES - heads))[None, :],
        "g_mix_pre": g_mix_pre, "g_mix_post": g_mix_post,
        "g_ffn_pre": g_ffn_pre, "g_ffn_post": g_ffn_post,
        "mlp_f32": (w_up, w_down), "late_f32": (w_kvf, w_q, w_o),
    }

    n_p = bp * t
    tm_p = min(t, 1024)
    h_p, u_p, (k_p, k16), (v_p, v16), lf_p, q_p, c_p = _layer0_and_kv(
        x_prompt.reshape(n_p, d), t, None, True, BF16, w, tm=tm_p)
    conv_prompt = u_p.reshape(bp, t // tm_p, SUBLANES, d)[:, -1, SUBLANES - (CONV_W - 1):, :][None]
    c3 = c_p.reshape(bp, t, LANES)
    tq = 512
    names = ("w_up1", "w_down1")
    cast = _side_cast(w, names, w["mlp_f32"], 1, bp * (t // tq) ** 2)
    o_p, *done = _fox_prompt(q_p.reshape(bp, t, d), k16.reshape(bp, t, d), v16.reshape(bp, t, d),
                             c3, jnp.swapaxes(c3[:, :, :heads], 1, 2), tq=tq, tk=tq,
                             cast=cast, cast_layer=1)
    w.update(zip(names, done))
    h_p, a_p = _outproj_residual(o_p.reshape(n_p, d), w["w_o"], h_p, w["g_mix_post"][1:2],
                                 w["g_ffn_pre"][1:2], tm=512)
    (y_p,) = _mlp_residual(a_p, w["w_up1"], w["w_down1"], h_p, w["g_ffn_post"][1:2], None,
                           tm=min(n_p, 1024), tf=512)

    n_s = bs * ts
    prev = state_conv[0]
    zeros = jnp.zeros((bs, ts - 2, d), F32)
    p1 = jnp.concatenate([prev[:, 1:2], zeros, zeros[:, :1]], axis=1).reshape(n_s, d)
    p2 = jnp.concatenate([prev, zeros], axis=1).reshape(n_s, d)
    h_s, u_s, (k_s,), (v_s,), lf_s, q_s, c_s = _layer0_and_kv(
        x_sample.reshape(n_s, d), ts, (p1, p2), False, F32, w, tm=n_s)
    conv_sample = u_s.reshape(bs, ts, d)[:, ts - (CONV_W - 1):, :][None]
    n_pool, page = cache_logf.shape[:2]
    page_sfx = _page_suffix(cache_logf.reshape(n_pool, page * heads), heads=heads,
                            pages_per_step=64 if n_pool % 64 == 0 else n_pool)
    q_rows = q_s.reshape(bs, ts, heads, HEAD_DIM).transpose(0, 2, 1, 3).reshape(bs, heads * ts, HEAD_DIM)
    cn = c_s[:, :heads].reshape(bs, ts, heads)
    o_s = _fox_sample(page_table, q_rows,
                      k_s.reshape(bs, ts * heads, HEAD_DIM), v_s.reshape(bs, ts * heads, HEAD_DIM),
                      jnp.tile(cn, (1, 1, LANES // heads)), cn.reshape(bs, 1, ts * heads),
                      cache_k, cache_v, page_sfx, group=8, joint=SAMPLE_JOINT, t_new=ts)
    o_s = o_s.reshape(bs, heads, ts, HEAD_DIM).transpose(0, 2, 1, 3).reshape(n_s, d)
    h_s, a_s = _outproj_residual(o_s, w["w_o"], h_s, w["g_mix_post"][1:2], w["g_ffn_pre"][1:2], tm=n_s)
    (y_s,) = _mlp_residual(a_s, w["w_up1"], w["w_down1"], h_s, w["g_ffn_post"][1:2], None,
                           tm=n_s, tf=512)

    return (y_p.reshape(bp, t, d), y_s.reshape(bs, ts, d), conv_prompt, conv_sample,
            k_p.reshape(bp, t, heads, HEAD_DIM), v_p.reshape(bp, t, heads, HEAD_DIM),
            lf_p[:, :heads].reshape(bp, t, heads),
            k_s.reshape(bs, ts, heads, HEAD_DIM), v_s.reshape(bs, ts, heads, HEAD_DIM),
            lf_s[:, :heads].reshape(bs, ts, heads))
```

```python
import functools
import math

import jax
import jax.numpy as jnp
from jax import lax
from jax.experimental import pallas as pl
from jax.experimental.pallas import tpu as pltpu

RMS_EPS = 1e-6
CONV_W = 3
N_HEADS = 16
HEAD_DIM = 128
LANES = 128
SUBLANES = 8
VMEM_LIMIT = 60 * 1024 * 1024
NEG_BIG = -1e30
LOG2E = math.log2(math.e)
SAMPLE_JOINT = 4

F32 = jnp.float32
BF16 = jnp.bfloat16


def _params(*sem):
    return pltpu.CompilerParams(dimension_semantics=sem, vmem_limit_bytes=VMEM_LIMIT)


def _rms(x, g):
    return x * lax.rsqrt(jnp.mean(x * x, axis=-1, keepdims=True) + RMS_EPS) * g


def _split3(x):
    a = x.astype(BF16)
    r = x - a.astype(F32)
    b = r.astype(BF16)
    c = (r - b.astype(F32)).astype(BF16)
    return a, b, c


def _dot01_exact(m01, x):
    a, b, c = _split3(x)
    d = functools.partial(jnp.dot, preferred_element_type=F32)
    return d(m01, a) + d(m01, b) + d(m01, c)


def _cast_specs(stacks, layer, n_steps, step_of):
    in_specs, out_specs, out_shape = [], [], []
    for stack in stacks:
        _, r, c = stack.shape
        rows = r // n_steps
        assert rows * n_steps == r and rows % (2 * SUBLANES) == 0
        in_specs.append(pl.BlockSpec((None, rows, c), lambda *g: (layer, step_of(*g), 0)))
        out_specs.append(pl.BlockSpec((rows, c), lambda *g: (step_of(*g), 0)))
        out_shape.append(jax.ShapeDtypeStruct((r, c), BF16))
    return in_specs, out_specs, out_shape


def _can_cast_in(stacks, n_steps):
    return all(s.shape[1] % (n_steps * 2 * SUBLANES) == 0 for s in stacks)


def _inproj_conv_kernel(*refs, tm, tiles_per_seq, seq_len, inject, n_cast):
    n_in = 8 if inject else 6
    cast_in = refs[n_in:n_in + n_cast]
    cast_out = refs[n_in + n_cast + 2:n_in + 2 * n_cast + 2]
    core = refs[:n_in] + refs[n_in + n_cast:n_in + n_cast + 2] + refs[n_in + 2 * n_cast + 2:]
    if inject:
        (x_ref, g_ref, wb_ref, wc_ref, wx_ref, wconv_ref, p1_ref, p2_ref,
         gated_ref, u_ref, a_s) = core
    else:
        (x_ref, g_ref, wb_ref, wc_ref, wx_ref, wconv_ref,
         gated_ref, u_ref, a_s, carry_s) = core
    i = pl.program_id(0)
    j = pl.program_id(1)
    for src, dst in zip(cast_in, cast_out):
        dst[...] = src[...].astype(BF16)

    @pl.when(j == 0)
    def _():
        a_s[...] = _rms(x_ref[...], g_ref[...]).astype(BF16)

    a = a_s[...]
    dot = functools.partial(jnp.dot, preferred_element_type=F32)
    u = dot(a, wc_ref[...]) * dot(a, wx_ref[...])
    rolled1 = pltpu.roll(u, 1, axis=0)
    rolled2 = pltpu.roll(u, 2, axis=0)
    row = lax.broadcasted_iota(jnp.int32, u.shape, 0)
    if inject:
        pos = row % seq_len
        u1 = jnp.where(pos >= 1, rolled1, p1_ref[...])
        u2 = jnp.where(pos >= 2, rolled2, p2_ref[...])
        u_ref[...] = u
    else:
        @pl.when(i % tiles_per_seq == 0)
        def _():
            carry_s[j] = jnp.zeros(carry_s.shape[1:], F32)

        tail = carry_s[j]
        t1 = tail[SUBLANES - 1:SUBLANES, :]
        t2 = tail[SUBLANES - 2:SUBLANES - 1, :]
        u1 = jnp.where(row == 0, t1, rolled1)
        u2 = jnp.where(row == 0, t2, jnp.where(row == 1, t1, rolled2))
        last = u[tm - SUBLANES:, :]
        carry_s[j] = last
        u_ref[...] = last
    wconv = wconv_ref[...]
    conv = wconv[0:1, :] * u2 + wconv[1:2, :] * u1 + wconv[2:3, :] * u
    gated_ref[...] = (dot(a, wb_ref[...]) * conv).astype(BF16)


def _inproj_conv(x, g, w_in, w_conv, *, tm, tn, seq_len, hist=None, cast=(), cast_layer=0):
    n, d = x.shape
    nj = d // tn
    c_in, c_out, c_shape = _cast_specs(cast, cast_layer, (n // tm) * nj, lambda i, j: i * nj + j)
    inject = hist is not None
    assert n % tm == 0 and d % tn == 0
    if not inject:
        assert seq_len % tm == 0
    x_spec = pl.BlockSpec((tm, d), lambda i, j: (i, 0))
    g_spec = pl.BlockSpec((1, d), lambda i, j: (0, 0))
    w_specs = [pl.BlockSpec((d, tn), lambda i, j, s=s: (0, j + s * nj)) for s in range(3)]
    wconv_spec = pl.BlockSpec((CONV_W, tn), lambda i, j: (0, j))
    tile_spec = pl.BlockSpec((tm, tn), lambda i, j: (i, j))
    in_specs = [x_spec, g_spec, *w_specs, wconv_spec]
    args = [x, g, w_in, w_in, w_in, w_conv]
    scratch = [pltpu.VMEM((tm, d), BF16)]
    if inject:
        in_specs += [tile_spec, tile_spec]
        args += list(hist)
        u_shape = jax.ShapeDtypeStruct((n, d), F32)
        u_spec = tile_spec
    else:
        scratch.append(pltpu.VMEM((nj, SUBLANES, tn), F32))
        u_shape = jax.ShapeDtypeStruct((n // tm * SUBLANES, d), F32)
        u_spec = pl.BlockSpec((SUBLANES, tn), lambda i, j: (i, j))
    kern = functools.partial(_inproj_conv_kernel, tm=tm, tiles_per_seq=max(seq_len // tm, 1),
                             seq_len=seq_len, inject=inject, n_cast=len(cast))
    return pl.pallas_call(
        kern,
        grid=(n // tm, nj),
        in_specs=in_specs + c_in,
        out_specs=[tile_spec, u_spec] + c_out,
        out_shape=[jax.ShapeDtypeStruct((n, d), BF16), u_shape] + c_shape,
        scratch_shapes=scratch,
        compiler_params=_params("arbitrary", "arbitrary"),
        name="inproj_conv",
    )(*args, *cast)


def _outproj_kernel(a_ref, w_ref, h_ref, gpost_ref, gnext_ref, h1_ref, anext_ref):
    mix = jnp.dot(a_ref[...], w_ref[...], preferred_element_type=F32)
    h1 = h_ref[...] + _rms(mix, gpost_ref[...])
    h1_ref[...] = h1
    anext_ref[...] = _rms(h1, gnext_ref[...]).astype(BF16)


def _outproj_residual(a, w, h, g_post, g_next, *, tm):
    n, d = h.shape
    row = pl.BlockSpec((tm, d), lambda i: (i, 0))
    vec = pl.BlockSpec((1, d), lambda i: (0, 0))
    return pl.pallas_call(
        _outproj_kernel,
        grid=(n // tm,),
        in_specs=[row, pl.BlockSpec((d, d), lambda i: (0, 0)), row, vec, vec],
        out_specs=[row, row],
        out_shape=[jax.ShapeDtypeStruct((n, d), F32), jax.ShapeDtypeStruct((n, d), BF16)],
        compiler_params=_params("arbitrary"),
        name="outproj_residual",
    )(a, w, h, g_post, g_next)


def _mlp_kernel(*refs, n_next, own_acc):
    a_ref, wup_ref, wdown_ref, h_ref, gpost_ref, gnext_ref, h2_ref = refs[:7]
    next_refs = refs[7:7 + n_next]
    acc = refs[7 + n_next] if own_acc else h2_ref
    f = pl.program_id(1)

    @pl.when(f == 0)
    def _():
        acc[...] = jnp.zeros(acc.shape, F32)

    hid = jnp.dot(a_ref[...], wup_ref[...], preferred_element_type=F32)
    hid = jnp.square(jnp.maximum(hid, 0.0)).astype(BF16)
    acc[...] += jnp.dot(hid, wdown_ref[...], preferred_element_type=F32)

    @pl.when(f == pl.num_programs(1) - 1)
    def _():
        h2 = h_ref[...] + _rms(acc[...], gpost_ref[...])
        h2_ref[...] = h2
        for k, ref in enumerate(next_refs):
            ref[...] = _rms(h2, gnext_ref[k:k + 1, :]).astype(BF16)


def _mlp_residual(a, w_up, w_down, h, g_post, g_next, *, tm, tf):
    n, d = h.shape
    d_ff = w_up.shape[1]
    n_next = 0 if g_next is None else g_next.shape[0]
    g_arr = jnp.ones((1, d), F32) if g_next is None else g_next
    own_acc = n_next > 0
    row = pl.BlockSpec((tm, d), lambda i, f: (i, 0))
    outs = [jax.ShapeDtypeStruct((n, d), F32)] + [jax.ShapeDtypeStruct((n, d), BF16)] * n_next
    return pl.pallas_call(
        functools.partial(_mlp_kernel, n_next=n_next, own_acc=own_acc),
        grid=(n // tm, d_ff // tf),
        in_specs=[row,
                  pl.BlockSpec((d, tf), lambda i, f: (0, f)),
                  pl.BlockSpec((tf, d), lambda i, f: (f, 0)),
                  row,
                  pl.BlockSpec((1, d), lambda i, f: (0, 0)),
                  pl.BlockSpec(g_arr.shape, lambda i, f: (0, 0))],
        out_specs=[row] * (1 + n_next),
        out_shape=outs,
        scratch_shapes=[pltpu.VMEM((tm, d), F32)] if own_acc else [],
        compiler_params=_params("arbitrary", "arbitrary"),
        name="mlp_residual",
    )(a, w_up, w_down, h, g_post, g_arr)


def _proj_kernel(a_ref, w_ref, *out_refs, scale):
    y = jnp.dot(a_ref[...], w_ref[...], preferred_element_type=F32)
    if scale is not None:
        y = y * scale
    for ref in out_refs:
        if len(ref.shape) == 3:
            per_head = [y[:, hh * HEAD_DIM:(hh + 1) * HEAD_DIM] for hh in range(ref.shape[1])]
            ref[...] = jnp.swapaxes(jnp.stack(per_head, axis=0), 0, 1).astype(ref.dtype)
        else:
            ref[...] = y.astype(ref.dtype)


def _proj(a, w, *, col0, n_out, out_dtypes, tm, tn, scale=None, split_heads=()):
    n, d = a.shape
    assert col0 % tn == 0 and n_out % tn == 0 and tn % HEAD_DIM == 0
    jb = col0 // tn
    tile = pl.BlockSpec((tm, tn), lambda i, j: (i, j))
    head_tile = pl.BlockSpec((tm, tn // HEAD_DIM, HEAD_DIM), lambda i, j: (i, j, 0))
    out_specs, out_shape = [], []
    for k, dt in enumerate(out_dtypes):
        if k in split_heads:
            out_specs.append(head_tile)
            out_shape.append(jax.ShapeDtypeStruct((n, n_out // HEAD_DIM, HEAD_DIM), dt))
        else:
            out_specs.append(tile)
            out_shape.append(jax.ShapeDtypeStruct((n, n_out), dt))
    return pl.pallas_call(
        functools.partial(_proj_kernel, scale=scale),
        grid=(n // tm, n_out // tn),
        in_specs=[pl.BlockSpec((tm, d), lambda i, j: (i, 0)),
                  pl.BlockSpec((d, tn), lambda i, j: (0, j + jb))],
        out_specs=out_specs,
        out_shape=out_shape,
        compiler_params=_params("arbitrary", "arbitrary"),
        name="proj",
    )(a, w)


def _logf_kernel(a_ref, wf_ref, bf_ref, lf_ref, c_ref, carry_s, *, tm, tiles_per_seq, seq_len):
    i = pl.program_id(0)
    x = jnp.dot(a_ref[...], wf_ref[...], preferred_element_type=F32) + bf_ref[...]
    lf = jnp.minimum(x, 0.0) - jnp.log1p(jnp.exp(-jnp.abs(x)))
    lf_ref[...] = lf
    r = lax.broadcasted_iota(jnp.int32, (tm, tm), 0)
    c = lax.broadcasted_iota(jnp.int32, (tm, tm), 1)
    same_seq = (r // seq_len) == (c // seq_len) if seq_len < tm else (r >= 0)
    tri = jnp.where((c <= r) & same_seq, 1.0, 0.0).astype(BF16)
    csum = _dot01_exact(tri, lf)

    @pl.when(i % tiles_per_seq == 0)
    def _():
        carry_s[...] = jnp.zeros(carry_s.shape, F32)

    csum = csum + carry_s[0:1, :]
    c_ref[...] = csum
    carry_s[...] = jnp.broadcast_to(csum[tm - 1:tm, :], carry_s.shape)


def _logf(a, w_f, b_f, *, tm, seq_len):
    n, d = a.shape
    assert seq_len % tm == 0 or tm % seq_len == 0
    tile = pl.BlockSpec((tm, LANES), lambda i: (i, 0))
    kern = functools.partial(_logf_kernel, tm=tm, tiles_per_seq=max(seq_len // tm, 1), seq_len=seq_len)
    return pl.pallas_call(
        kern,
        grid=(n // tm,),
        in_specs=[pl.BlockSpec((tm, d), lambda i: (i, 0)),
                  pl.BlockSpec((d, LANES), lambda i: (0, 0)),
                  pl.BlockSpec((1, LANES), lambda i: (0, 0))],
        out_specs=[tile, tile],
        out_shape=[jax.ShapeDtypeStruct((n, LANES), F32)] * 2,
        scratch_shapes=[pltpu.VMEM((SUBLANES, LANES), F32)],
        compiler_params=_params("arbitrary"),
        name="logf",
    )(a, w_f, b_f)


def _online_softmax_update(chunks, cq, m_old, l_old):
    mx = functools.reduce(jnp.maximum, chunks)
    rowmax = jnp.broadcast_to(jnp.max(mx, axis=-1, keepdims=True), mx.shape)
    m_new = jnp.maximum(m_old, rowmax + cq)
    alpha = jnp.exp2(m_old - m_new)
    shift = m_new - cq
    p = [jnp.exp2(c - shift) for c in chunks]
    rowsum = jnp.broadcast_to(jnp.sum(functools.reduce(jnp.add, p), axis=-1, keepdims=True), mx.shape)
    return m_new, alpha * l_old + rowsum, alpha, p


def _fox_prompt_kernel(q_ref, k_ref, v_ref, cq_ref, ck_ref, *refs, tq, tk, n_cast):
    cast_in = refs[:n_cast]
    o_ref = refs[n_cast]
    cast_out = refs[n_cast + 1:2 * n_cast + 1]
    m_s, l_s, acc_s, cqb_s = refs[2 * n_cast + 1:]
    qi = pl.program_id(1)
    ki = pl.program_id(2)
    n_chunks = tk // LANES
    for src, dst in zip(cast_in, cast_out):
        dst[...] = src[...].astype(BF16)

    @pl.when(ki == 0)
    def _():
        m_s[...] = jnp.full(m_s.shape, -jnp.inf, F32)
        l_s[...] = jnp.zeros(l_s.shape, F32)
        acc_s[...] = jnp.zeros(acc_s.shape, F32)
        cq = cq_ref[0]
        for h in range(N_HEADS):
            cqb_s[h] = jnp.broadcast_to(cq[:, h:h + 1], cqb_s.shape[1:])

    def step(on_diagonal):
        ck = ck_ref[0]
        if on_diagonal:
            row = lax.broadcasted_iota(jnp.int32, (tq, LANES), 0)
            lane = lax.broadcasted_iota(jnp.int32, (tq, LANES), 1)
        for h in range(N_HEADS):
            cols = slice(h * HEAD_DIM, (h + 1) * HEAD_DIM)
            s = lax.dot_general(q_ref[0, :, cols], k_ref[0, :, cols],
                                (((1,), (1,)), ((), ())), preferred_element_type=F32)
            chunks = []
            for c in range(n_chunks):
                lanes = slice(c * LANES, (c + 1) * LANES)
                sc = s[:, lanes] - ck[h:h + 1, lanes]
                if on_diagonal:
                    sc = jnp.where(lane + c * LANES <= row, sc, NEG_BIG)
                chunks.append(sc)
            mx = functools.reduce(jnp.maximum, chunks)
            rowmax = jnp.broadcast_to(jnp.max(mx, axis=-1, keepdims=True), mx.shape)
            cqh = cqb_s[h]
            m_old = m_s[h]
            m_new = jnp.maximum(m_old, rowmax + cqh)
            alpha = jnp.exp2(m_old - m_new)
            shift = m_new - cqh
            p = [jnp.exp2(c - shift).astype(BF16) for c in chunks]
            v_ext = jnp.concatenate([v_ref[0, :, cols], jnp.ones((tk, LANES), BF16)], axis=1)
            pv = jnp.dot(jnp.concatenate(p, axis=1), v_ext, preferred_element_type=F32)
            m_s[h] = m_new
            l_s[h] = alpha * l_s[h] + pv[:, LANES:]
            acc_s[:, cols] = alpha * acc_s[:, cols] + pv[:, :LANES]

    pl.when(ki < qi)(functools.partial(step, False))
    pl.when(ki == qi)(functools.partial(step, True))

    @pl.when(ki == pl.num_programs(2) - 1)
    def _():
        for h in range(N_HEADS):
            cols = slice(h * HEAD_DIM, (h + 1) * HEAD_DIM)
            o_ref[0, :, cols] = (acc_s[:, cols] / l_s[h]).astype(o_ref.dtype)


def _fox_prompt(q, k, v, c_rows, c_heads, *, tq, tk, cast=(), cast_layer=0):
    b, t, d = q.shape
    assert tq == tk
    nq, nk = t // tq, t // tk
    c_in, c_out, c_shape = _cast_specs(cast, cast_layer, b * nq * nk,
                                       lambda bi, qi, ki: (bi * nq + qi) * nk + ki)
    kv_map = lambda bi, qi, ki: (bi, jnp.minimum(ki, qi), 0)
    return pl.pallas_call(
        functools.partial(_fox_prompt_kernel, tq=tq, tk=tk, n_cast=len(cast)),
        grid=(b, nq, nk),
        in_specs=[pl.BlockSpec((1, tq, d), lambda bi, qi, ki: (bi, qi, 0)),
                  pl.BlockSpec((1, tk, d), kv_map),
                  pl.BlockSpec((1, tk, d), kv_map),
                  pl.BlockSpec((1, tq, LANES), lambda bi, qi, ki: (bi, qi, 0)),
                  pl.BlockSpec((1, N_HEADS, tk), lambda bi, qi, ki: (bi, 0, jnp.minimum(ki, qi))),
                  *c_in],
        out_specs=[pl.BlockSpec((1, tq, d), lambda bi, qi, ki: (bi, qi, 0)), *c_out],
        out_shape=[jax.ShapeDtypeStruct((b, t, d), BF16), *c_shape],
        scratch_shapes=[pltpu.VMEM((N_HEADS, tq, LANES), F32),
                        pltpu.VMEM((N_HEADS, tq, LANES), F32),
                        pltpu.VMEM((tq, d), F32),
                        pltpu.VMEM((N_HEADS, tq, LANES), F32)],
        compiler_params=_params("arbitrary", "arbitrary", "arbitrary"),
        name="fox_prompt",
    )(q, k, v, c_rows, c_heads, *cast)


def _page_suffix_kernel(lf_ref, out_ref, *, heads):
    x = lf_ref[...]
    width = x.shape[1]
    lane = lax.broadcasted_iota(jnp.int32, x.shape, 1)
    incl = x
    k = heads
    while k < width:
        incl = incl + jnp.where(lane < width - k, pltpu.roll(incl, width - k, axis=1), 0.0)
        k *= 2
    total = jnp.where(lane < heads, incl, 0.0)
    k = heads
    while k < width:
        total = total + pltpu.roll(total, k, axis=1)
        k *= 2
    out_ref[:, 0, :] = (incl - x) * LOG2E
    out_ref[:, 1, :] = total * LOG2E


def _page_suffix(logf_flat, *, heads, pages_per_step):
    n_pool, width = logf_flat.shape
    assert n_pool % pages_per_step == 0
    return pl.pallas_call(
        functools.partial(_page_suffix_kernel, heads=heads),
        grid=(n_pool // pages_per_step,),
        in_specs=[pl.BlockSpec((pages_per_step, width), lambda i: (i, 0))],
        out_specs=pl.BlockSpec((pages_per_step, 2, width), lambda i: (i, 0, 0)),
        out_shape=jax.ShapeDtypeStruct((n_pool, 2, width), F32),
        compiler_params=_params("arbitrary"),
        name="page_suffix",
    )(logf_flat)


def _fox_sample_step(step, n_steps, q_ref, kn_ref, vn_ref, cq_ref, cnrow_ref, k_refs, v_refs, s_refs,
                     o_ref, m_s, l_s, acc_s, carry_s, *, joint, t_new):
    group = len(k_refs)
    rows = q_ref.shape[1]
    heads = rows // t_new
    width = carry_s.shape[1]
    nt = (((1,), (1,)), ((), ()))
    q = q_ref[0]
    cq = cq_ref[0]

    def own_head(n_lanes):
        lane_head = lax.broadcasted_iota(jnp.int32, (t_new, n_lanes), 1) % heads
        return [lane_head == h for h in range(heads)]

    def compact(s, masks):
        out = s[(heads - 1) * t_new:, :]
        for h in range(heads - 2, -1, -1):
            out = jnp.where(masks[h], s[h * t_new:(h + 1) * t_new, :], out)
        return out

    def expand(p, masks):
        return jnp.concatenate([jnp.where(masks[h], p, 0.0) for h in range(heads)], axis=0)

    def per_head_rows(x):
        return jnp.concatenate([jnp.broadcast_to(x[:, h:h + 1], (t_new, LANES)) for h in range(heads)],
                               axis=0)

    def over_positions(x, op):
        k = heads
        while k < LANES:
            x = op(x, pltpu.roll(x, k, axis=1))
            k *= 2
        return x

    def update(blocks):
        tiles = [c[:, i * LANES:(i + 1) * LANES] for c, _, _ in blocks for i in range(c.shape[1] // LANES)]
        m_old = m_s[...]
        m_new = jnp.maximum(m_old, over_positions(functools.reduce(jnp.maximum, tiles), jnp.maximum) + cq)
        alpha = jnp.exp2(m_old - m_new)
        shift = m_new - cq
        acc = per_head_rows(alpha) * acc_s[...]
        l_add = jnp.zeros_like(m_old)
        start = 0
        for c, v, masks in blocks:
            n = c.shape[1] // LANES
            p_tiles = [jnp.exp2(t - shift) for t in tiles[start:start + n]]
            start += n
            l_add = l_add + functools.reduce(jnp.add, p_tiles)
            acc = acc + jnp.dot(expand(jnp.concatenate(p_tiles, axis=1), masks), v,
                                preferred_element_type=F32)
        m_s[...] = m_new
        l_s[...] = alpha * l_s[...] + over_positions(l_add, jnp.add)
        acc_s[...] = acc

    @pl.when(step == 0)
    def _():
        m_s[...] = jnp.full(m_s.shape, -jnp.inf, F32)
        l_s[...] = jnp.zeros(l_s.shape, F32)
        acc_s[...] = jnp.zeros(acc_s.shape, F32)
        carry_s[...] = jnp.zeros(carry_s.shape, F32)
        masks = own_head(LANES)
        s = compact(lax.dot_general(q, kn_ref[0], nt, preferred_element_type=F32), masks) - cnrow_ref[0]
        t_q = lax.broadcasted_iota(jnp.int32, s.shape, 0)
        t_k = lax.broadcasted_iota(jnp.int32, s.shape, 1) // heads
        update([(jnp.where(t_k <= t_q, s, NEG_BIG), vn_ref[0], masks)])

    masks = own_head(width)
    carry = carry_s[...]
    for g0 in range(0, group, joint):
        blocks = []
        for g in range(g0, g0 + joint):
            sfx = s_refs[g][0]
            kf = k_refs[g][0].reshape(width, HEAD_DIM)
            vf = v_refs[g][0].reshape(width, HEAD_DIM)
            s = compact(lax.dot_general(q, kf, nt, preferred_element_type=F32), masks)
            blocks.append((s + (carry + sfx[0:1, :]), vf, masks))
            carry = carry + sfx[1:2, :]
        update(blocks)
    carry_s[...] = carry

    @pl.when(step == n_steps - 1)
    def _():
        o_ref[0] = (acc_s[...] / per_head_rows(l_s[...])).astype(o_ref.dtype)


def _fox_sample_kernel(pt_ref, q_ref, kn_ref, vn_ref, cq_ref, cnrow_ref, *refs, group, joint, t_new):
    k_refs = refs[:group]
    v_refs = refs[group:2 * group]
    s_refs = refs[2 * group:3 * group]
    o_ref = refs[3 * group]
    _fox_sample_step(pl.program_id(1), pl.num_programs(1), q_ref, kn_ref, vn_ref, cq_ref, cnrow_ref,
                     k_refs, v_refs, s_refs, o_ref, *refs[3 * group + 1:], joint=joint, t_new=t_new)


def _fox_sample(page_table, q, k_new, v_new, cq_tile, cn_row, cache_k, cache_v, page_sfx,
                *, group, joint, t_new):
    b, rows, dh = q.shape
    n_pages = page_table.shape[1]
    _, page, heads, _ = cache_k.shape
    width = page * heads
    assert n_pages % group == 0 and group % joint == 0 and rows == heads * t_new and rows == LANES
    assert LANES % heads == 0

    def page_map(g, ndim):
        return lambda bi, si, pt: (pt[bi, n_pages - 1 - (si * group + g)],) + (0,) * (ndim - 1)

    per_seq = lambda shape: pl.BlockSpec((1,) + shape, lambda bi, si, pt: (bi, 0, 0))
    kv_specs = [pl.BlockSpec((1, page, heads, dh), page_map(g, 4)) for g in range(group)]
    sfx_specs = [pl.BlockSpec((1, 2, width), page_map(g, 3)) for g in range(group)]
    stat = pltpu.VMEM((t_new, LANES), F32)
    grid_spec = pltpu.PrefetchScalarGridSpec(
        num_scalar_prefetch=1,
        grid=(b, n_pages // group),
        in_specs=[per_seq((rows, dh)), per_seq((rows, dh)), per_seq((rows, dh)),
                  per_seq((t_new, LANES)), per_seq((1, rows)),
                  *kv_specs, *kv_specs, *sfx_specs],
        out_specs=per_seq((rows, dh)),
        scratch_shapes=[stat, stat, pltpu.VMEM((rows, dh), F32), pltpu.VMEM((1, width), F32)],
    )
    return pl.pallas_call(
        functools.partial(_fox_sample_kernel, group=group, joint=joint, t_new=t_new),
        grid_spec=grid_spec,
        out_shape=jax.ShapeDtypeStruct((b, rows, dh), BF16),
        compiler_params=_params("arbitrary", "arbitrary"),
        name="fox_sample",
    )(page_table, q, k_new, v_new, cq_tile, cn_row,
      *([cache_k] * group), *([cache_v] * group), *([page_sfx] * group))


def _layer0_and_kv(x, seq_len, hist, kv16, q_dtype, w, mlp_w, *, tm):
    n, d = x.shape
    tn = 512
    cast = ()
    if mlp_w is None:
        if _can_cast_in(w["mlp_f32"], (n // tm) * (d // tn)):
            cast = w["mlp_f32"]
        else:
            mlp_w = tuple(s[0].astype(BF16) for s in w["mlp_f32"])
    gated, u_rows, *cast_out = _inproj_conv(x, w["g_mix_pre"][0:1], w["w_in"], w["w_conv"], tm=tm, tn=tn,
                                            seq_len=seq_len, hist=hist, cast=cast, cast_layer=0)
    if cast:
        mlp_w = tuple(cast_out)
    row_tm = min(n, 512)
    h, a = _outproj_residual(gated, w["w_out"], x, w["g_mix_post"][0:1], w["g_ffn_pre"][0:1], tm=row_tm)
    g_next = jnp.concatenate([w["g_kv"], w["g_mix_pre"][1:2]], axis=0)
    h, a_kv, a_q = _mlp_residual(a, *mlp_w, h, w["g_ffn_post"][0:1], g_next, tm=row_tm, tf=512)
    ptm = min(n, 1024)
    kv_dtypes = (F32, BF16) if kv16 else (F32,)
    k = _proj(a_kv, w["w_kvf"], col0=0, n_out=d, out_dtypes=kv_dtypes, tm=ptm, tn=1024, split_heads=(0,))
    v = _proj(a_kv, w["w_kvf"], col0=d, n_out=d, out_dtypes=kv_dtypes, tm=ptm, tn=1024, split_heads=(0,))
    lf, c = _logf(a_kv, w["w_f"], w["b_f"], tm=min(n, 256), seq_len=seq_len)
    (q,) = _proj(a_q, w["w_q"], col0=0, n_out=d, out_dtypes=(q_dtype,), tm=ptm, tn=1024,
                 scale=LOG2E * HEAD_DIM ** -0.5)
    return h, u_rows, k, v, lf, q, c * LOG2E, mlp_w


def kernel(x_prompt, x_sample, state_conv, cache_k, cache_v, cache_logf, page_table,
           w_in_a, w_conv_a, w_out_a, g_kv, w_kvf, b_f, w_q, w_o,
           g_mix_pre, g_mix_post, g_ffn_pre, g_ffn_post, w_up, w_down):
    bp, t, d = x_prompt.shape
    bs, ts, _ = x_sample.shape
    heads = N_HEADS
    w = {
        "w_in": w_in_a[0].astype(BF16), "w_conv": w_conv_a[0], "w_out": w_out_a[0].astype(BF16),
        "g_kv": g_kv[None, :], "w_kvf": w_kvf.astype(BF16),
        "w_f": jnp.pad(w_kvf[:, 2 * d:], ((0, 0), (0, LANES - heads))).astype(BF16),
        "b_f": jnp.pad(b_f, (0, LANES - heads))[None, :],
        "w_q": w_q[0].astype(BF16), "w_o": w_o[0].astype(BF16),
        "g_mix_pre": g_mix_pre, "g_mix_post": g_mix_post,
        "g_ffn_pre": g_ffn_pre, "g_ffn_post": g_ffn_post,
        "mlp_f32": (w_up, w_down),
    }

    n_p = bp * t
    tm_p = min(t, 1024)
    tq = 512
    h_p, u_p, (k_p, k16), (v_p, v16), lf_p, q_p, c_p, mlp0 = _layer0_and_kv(
        x_prompt.reshape(n_p, d), t, None, True, BF16, w, None, tm=tm_p)
    conv_prompt = u_p.reshape(bp, t // tm_p, SUBLANES, d)[:, -1, SUBLANES - (CONV_W - 1):, :][None]
    c3 = c_p.reshape(bp, t, LANES)
    cast1 = w["mlp_f32"] if _can_cast_in(w["mlp_f32"], bp * (t // tq) ** 2) else ()
    o_p, *mlp1 = _fox_prompt(q_p.reshape(bp, t, d), k16.reshape(bp, t, d), v16.reshape(bp, t, d),
                             c3, jnp.swapaxes(c3[:, :, :heads], 1, 2), tq=tq, tk=tq,
                             cast=cast1, cast_layer=1)
    if not cast1:
        mlp1 = [s[1].astype(BF16) for s in w["mlp_f32"]]
    h_p, a_p = _outproj_residual(o_p.reshape(n_p, d), w["w_o"], h_p, w["g_mix_post"][1:2],
                                 w["g_ffn_pre"][1:2], tm=512)
    (y_p,) = _mlp_residual(a_p, *mlp1, h_p, w["g_ffn_post"][1:2], None, tm=min(n_p, 1024), tf=512)

    n_s = bs * ts
    prev = state_conv[0]
    zeros = jnp.zeros((bs, ts - 2, d), F32)
    p1 = jnp.concatenate([prev[:, 1:2], zeros, zeros[:, :1]], axis=1).reshape(n_s, d)
    p2 = jnp.concatenate([prev, zeros], axis=1).reshape(n_s, d)
    h_s, u_s, (k_s,), (v_s,), lf_s, q_s, c_s, _ = _layer0_and_kv(
        x_sample.reshape(n_s, d), ts, (p1, p2), False, F32, w, mlp0, tm=n_s)
    conv_sample = u_s.reshape(bs, ts, d)[:, ts - (CONV_W - 1):, :][None]
    n_pool, page = cache_logf.shape[:2]
    page_sfx = _page_suffix(cache_logf.reshape(n_pool, page * heads), heads=heads,
                            pages_per_step=64 if n_pool % 64 == 0 else n_pool)
    q_rows = q_s.reshape(bs, ts, heads, HEAD_DIM).transpose(0, 2, 1, 3).reshape(bs, heads * ts, HEAD_DIM)
    cn = c_s[:, :heads].reshape(bs, ts, heads)
    o_s = _fox_sample(page_table, q_rows,
                      k_s.reshape(bs, ts * heads, HEAD_DIM), v_s.reshape(bs, ts * heads, HEAD_DIM),
                      jnp.tile(cn, (1, 1, LANES // heads)), cn.reshape(bs, 1, ts * heads),
                      cache_k, cache_v, page_sfx, group=8, joint=SAMPLE_JOINT, t_new=ts)
    o_s = o_s.reshape(bs, heads, ts, HEAD_DIM).transpose(0, 2, 1, 3).reshape(n_s, d)
    h_s, a_s = _outproj_residual(o_s, w["w_o"], h_s, w["g_mix_post"][1:2], w["g_ffn_pre"][1:2], tm=n_s)
    (y_s,) = _mlp_residual(a_s, *mlp1, h_s, w["g_ffn_post"][1:2], None, tm=n_s, tf=512)

    return (y_p.reshape(bp, t, d), y_s.reshape(bs, ts, d), conv_prompt, conv_sample,
            k_p.reshape(bp, t, heads, HEAD_DIM), v_p.reshape(bp, t, heads, HEAD_DIM),
            lf_p[:, :heads].reshape(bp, t, heads),
            k_s.reshape(bs, ts, heads, HEAD_DIM), v_s.reshape(bs, ts, heads, HEAD_DIM),
            lf_s[:, :heads].reshape(bs, ts, heads))
```
